```python
import math
import jax, jax.numpy as jnp
from jax import lax
import numpy as np

D_MODEL = 1024
BATCH = 8
SEQ = 4096
DEPTH = 2
DEC_BATCH = 32
DEC_SEQ = 1
PAST_LEN = 16384
PAGE_SIZE = 128

N_EVEN = (DEPTH + 1) // 2
N_ODD = DEPTH // 2
A_WIDTH = D_MODEL // 2
A_HEAD = 64
A_HEADS = A_WIDTH // A_HEAD
DECAY_LORA = 64
AAA_LORA = 64
GATE_LORA = 128
RWKV_PROJ = 3 * A_WIDTH + DECAY_LORA + AAA_LORA + GATE_LORA
SPLIT_A = (A_WIDTH, 2 * A_WIDTH, 3 * A_WIDTH, 3 * A_WIDTH + DECAY_LORA, 3 * A_WIDTH + DECAY_LORA + AAA_LORA)
GN_EPS = 64e-5
B_WIDTH = D_MODEL - A_WIDTH
S5_GROUP = 16
S5_GROUPS = B_WIDTH // S5_GROUP
S5_STATE = 64
PROJ_AB = RWKV_PROJ + B_WIDTH
C_HEAD = 64
C_HEADS = D_MODEL // C_HEAD
PROJ_C = 3 * D_MODEL + C_HEADS
Q_BLOCK = 128
FORGET_BIAS_LO = 4.0
FORGET_BIAS_HI = 10.0
D_FF = ((8 * D_MODEL + 3 * 256 - 1) // (3 * 256)) * 256
ALPHA = (2.0 * DEPTH) ** 0.25
BETA = (8.0 * DEPTH) ** -0.25
LN_EPS = 1e-5

kernel_name = 'hybrid_rwkv7_s5_fox_step'


def layer_norm(x, g, b):
    xf = x.astype(jnp.float32)
    mu = jnp.mean(xf, -1, keepdims=True)
    var = jnp.mean(jnp.square(xf - mu), -1, keepdims=True)
    y = (xf - mu) * lax.rsqrt(var + LN_EPS) * g.astype(jnp.float32) + b.astype(jnp.float32)
    return y.astype(x.dtype)


def swiglu(x, w_gate, w_up, w_down):
    return (jax.nn.silu(x @ w_gate) * (x @ w_up)) @ w_down


def rwkv7_mix(p, shift_prev, wkv0, mu, w0, w_lb, a0, a_lb, g_lb, k_k, k_a, r_k, lnx_g, lnx_b):
    bsz, t, _ = p.shape
    f32 = jnp.float32
    p = p.astype(f32)
    p_prev = jnp.concatenate([shift_prev.astype(f32)[:, None], p[:, :-1]], axis=1)
    ps = p + (p_prev - p) * mu.astype(f32)
    r, k, v, wl, al, gl = jnp.split(ps, SPLIT_A, axis=-1)
    w_log = -jax.nn.softplus(-(w0.astype(f32) + jnp.tanh(wl) @ w_lb.astype(f32))) - 0.5
    decay = jnp.exp(-jnp.exp(w_log))
    a = jax.nn.sigmoid(a0.astype(f32) + al @ a_lb.astype(f32))
    g = jax.nn.sigmoid(gl) @ g_lb.astype(f32)
    heads = lambda z: z.reshape(bsz, t, A_HEADS, A_HEAD)
    kk = heads(k * k_k.astype(f32))
    kk = kk / jnp.maximum(jnp.sqrt(jnp.sum(kk * kk, -1, keepdims=True)), 1e-12)
    k = k * (1.0 + (a - 1.0) * k_a.astype(f32))
    r_h, w_h, k_h, v_h, a_h = heads(r), heads(decay), heads(k), heads(v), heads(a)
    b_h = kk * a_h

    def step(S, inp):
        r_t, w_t, k_t, v_t, kk_t, b_t = inp
        sa = jnp.einsum('bhvk,bhk->bhv', S, -kk_t)
        S = S * w_t[:, :, None, :] + sa[..., None] * b_t[:, :, None, :] + v_t[..., None] * k_t[:, :, None, :]
        return S, jnp.einsum('bhvk,bhk->bhv', S, r_t)

    xs = tuple(jnp.moveaxis(z, 1, 0) for z in (r_h, w_h, k_h, v_h, kk, b_h))
    wkv_new, y = lax.scan(step, wkv0.astype(f32), xs)
    y = jnp.moveaxis(y, 0, 1)
    mean = jnp.mean(y, -1, keepdims=True)
    var = jnp.mean(jnp.square(y - mean), -1, keepdims=True)
    y = ((y - mean) * lax.rsqrt(var + GN_EPS)).reshape(bsz, t, A_WIDTH) * lnx_g.astype(f32) + lnx_b.astype(f32)
    bonus = jnp.sum(r_h * k_h * r_k.astype(f32), -1, keepdims=True) * v_h
    y = (y + bonus.reshape(bsz, t, A_WIDTH)) * g
    return y, p[:, -1], wkv_new


def _complex_affine_combine(e1, e2):
    a1r, a1i, b1r, b1i = e1
    a2r, a2i, b2r, b2i = e2
    return (a1r * a2r - a1i * a2i,
            a1r * a2i + a1i * a2r,
            a2r * b1r - a2i * b1i + b2r,
            a2r * b1i + a2i * b1r + b2i)


def s5_mix(u, h_re0, h_im0, lam_re, lam_im, log_dt, b_re, b_im, c_re, c_im, d_skip, w_glu, b_glu):
    bsz, t, _ = u.shape
    f32 = jnp.float32
    ug = u.astype(f32).reshape(bsz, t, S5_GROUPS, S5_GROUP)
    lr, li = lam_re.astype(f32), lam_im.astype(f32)
    dt = jnp.exp(log_dt.astype(f32))[:, None]
    mag = jnp.exp(lr * dt)
    ab_re, ab_im = mag * jnp.cos(li * dt), mag * jnp.sin(li * dt)
    den = lr * lr + li * li
    n_re = ab_re - 1.0
    f_re = (n_re * lr + ab_im * li) / den
    f_im = (ab_im * lr - n_re * li) / den
    br, bi = b_re.astype(f32), b_im.astype(f32)
    bb_re = f_re[..., None] * br - f_im[..., None] * bi
    bb_im = f_re[..., None] * bi + f_im[..., None] * br
    e_re = jnp.einsum('gpc,btgc->btgp', bb_re, ug)
    e_im = jnp.einsum('gpc,btgc->btgp', bb_im, ug)
    h_re0, h_im0 = h_re0.astype(f32), h_im0.astype(f32)
    e_re = e_re.at[:, 0].add(ab_re * h_re0 - ab_im * h_im0)
    e_im = e_im.at[:, 0].add(ab_re * h_im0 + ab_im * h_re0)
    a_re = jnp.broadcast_to(ab_re, (1, t) + ab_re.shape)
    a_im = jnp.broadcast_to(ab_im, (1, t) + ab_im.shape)
    _, _, h_re, h_im = lax.associative_scan(_complex_affine_combine, (a_re, a_im, e_re, e_im), axis=1)
    y = (jnp.einsum('gcp,btgp->btgc', c_re.astype(f32), h_re)
         - jnp.einsum('gcp,btgp->btgc', c_im.astype(f32), h_im)
         + d_skip.astype(f32) * ug)
    y = jax.nn.gelu(y.reshape(bsz, t, B_WIDTH))
    y = y * jax.nn.sigmoid(y @ w_glu.astype(f32) + b_glu.astype(f32))
    return y, h_re[:, -1], h_im[:, -1]


def mixer_ab(x, shift_prev, wkv0, s5_re0, s5_im0, w_in, mu, w0, w_lb, a0, a_lb, g_lb, k_k, k_a, r_k,
             lnx_g, lnx_b, lam_re, lam_im, log_dt, b_re, b_im, c_re, c_im, d_skip, w_glu, b_glu, w_out):
    proj = x @ w_in
    y_a, shift_new, wkv_new = rwkv7_mix(proj[..., :RWKV_PROJ], shift_prev, wkv0, mu, w0, w_lb, a0, a_lb,
                                        g_lb, k_k, k_a, r_k, lnx_g, lnx_b)
    y_b, re_new, im_new = s5_mix(proj[..., RWKV_PROJ:], s5_re0, s5_im0, lam_re, lam_im, log_dt,
                                 b_re, b_im, c_re, c_im, d_skip, w_glu, b_glu)
    out = jnp.concatenate([y_a, y_b], axis=-1).astype(x.dtype) @ w_out
    return out, shift_new, wkv_new, re_new, im_new


def _fox_project(x, w_in, b_f):
    bsz, t, _ = x.shape
    proj = x @ w_in
    q = proj[..., :D_MODEL].reshape(bsz, t, C_HEADS, C_HEAD)
    k = proj[..., D_MODEL:2 * D_MODEL].reshape(bsz, t, C_HEADS, C_HEAD)
    v = proj[..., 2 * D_MODEL:3 * D_MODEL].reshape(bsz, t, C_HEADS, C_HEAD)
    logf = jax.nn.log_sigmoid(proj[..., 3 * D_MODEL:].astype(jnp.float32) + b_f.astype(jnp.float32))
    return q, k, v, logf


def fox_prompt(x, w_in, b_f, w_out):
    bsz, t, _ = x.shape
    f32 = jnp.float32
    q, k, v, logf = _fox_project(x, w_in, b_f)
    c = jnp.cumsum(logf, axis=1).transpose(0, 2, 1)
    n_blk = t // Q_BLOCK
    qb = q.reshape(bsz, n_blk, Q_BLOCK, C_HEADS, C_HEAD).transpose(1, 0, 2, 3, 4)
    cb = c.reshape(bsz, C_HEADS, n_blk, Q_BLOCK).transpose(2, 0, 1, 3)
    kpos = jnp.arange(t)
    scale = C_HEAD ** -0.5

    def block(args):
        i, q_i, c_i = args
        s = jnp.einsum('bqhd,bkhd->bhqk', q_i, k).astype(f32) * scale
        s = s + c_i[..., None] - c[:, :, None, :]
        qpos = i * Q_BLOCK + jnp.arange(Q_BLOCK)
        s = jnp.where(kpos[None, :] <= qpos[:, None], s, -jnp.inf)
        p = jax.nn.softmax(s, axis=-1)
        return jnp.einsum('bhqk,bkhd->bqhd', p.astype(v.dtype), v)

    o = lax.map(block, (jnp.arange(n_blk), qb, cb))
    o = o.transpose(1, 0, 2, 3, 4).reshape(bsz, t, D_MODEL)
    return o @ w_out, k, v, logf.astype(x.dtype)


def fox_sample(x, cache_k, cache_v, cache_logf, layer_idx, page_table, w_in, b_f, w_out):
    bsz, t, _ = x.shape
    f32 = jnp.float32
    q, k, v, logf = _fox_project(x, w_in, b_f)
    k_past = cache_k[layer_idx][page_table]
    n_past = k_past.shape[1] * k_past.shape[2]
    k_past = k_past.reshape(bsz, n_past, C_HEADS, C_HEAD)
    v_past = cache_v[layer_idx][page_table].reshape(bsz, n_past, C_HEADS, C_HEAD)
    lf_past = cache_logf[layer_idx][page_table].reshape(bsz, n_past, C_HEADS).astype(f32)
    suffix = (lax.cumsum(lf_past, axis=1, reverse=True) - lf_past).transpose(0, 2, 1)
    c_new = jnp.cumsum(logf, axis=1).transpose(0, 2, 1)
    scale = C_HEAD ** -0.5
    s_past = (jnp.einsum('bqhd,bkhd->bhqk', q, k_past).astype(f32) * scale
              + c_new[..., None] + suffix[:, :, None, :])
    s_new = (jnp.einsum('bqhd,bkhd->bhqk', q, k).astype(f32) * scale
             + c_new[..., None] - c_new[:, :, None, :])
    causal = jnp.tril(jnp.ones((t, t), dtype=bool))
    s_new = jnp.where(causal, s_new, -jnp.inf)
    p = jax.nn.softmax(jnp.concatenate([s_past, s_new], axis=-1), axis=-1)
    o = (jnp.einsum('bhqk,bkhd->bqhd', p[..., :n_past].astype(v.dtype), v_past)
         + jnp.einsum('bhqk,bkhd->bqhd', p[..., n_past:].astype(v.dtype), v))
    return o.reshape(bsz, t, D_MODEL) @ w_out, k, v, logf.astype(x.dtype)


def setup_inputs(seed: int = 0) -> dict:
    key = jax.random.key(seed)
    ks = iter(jax.random.split(key, 64))
    f32 = jnp.float32
    nrm = lambda shape, scale: scale * jax.random.normal(next(ks), shape, f32)
    uni = lambda shape, lo, hi: jax.random.uniform(next(ks), shape, f32, lo, hi)
    n_pages = PAST_LEN // PAGE_SIZE
    n_used = DEC_BATCH * n_pages
    n_pool = (n_used * 5 + 3) // 4
    page_table = jax.random.permutation(next(ks), n_pool)[:n_used].reshape(DEC_BATCH, n_pages).astype(jnp.int32)
    inv = D_MODEL ** -0.5
    head_bias = jnp.linspace(FORGET_BIAS_LO, FORGET_BIAS_HI, C_HEADS, dtype=f32)
    return {
        'x_prompt': nrm((BATCH, SEQ, D_MODEL), 1.0),
        'x_sample': nrm((DEC_BATCH, DEC_SEQ, D_MODEL), 1.0),
        'state_shift': nrm((N_EVEN, DEC_BATCH, RWKV_PROJ), 1.0),
        'state_wkv': nrm((N_EVEN, DEC_BATCH, A_HEADS, A_HEAD, A_HEAD), 0.3),
        'state_s5_re': nrm((N_EVEN, DEC_BATCH, S5_GROUPS, S5_STATE), 0.3),
        'state_s5_im': nrm((N_EVEN, DEC_BATCH, S5_GROUPS, S5_STATE), 0.3),
        'cache_k': nrm((N_ODD, n_pool, PAGE_SIZE, C_HEADS, C_HEAD), 1.0),
        'cache_v': nrm((N_ODD, n_pool, PAGE_SIZE, C_HEADS, C_HEAD), 1.0),
        'cache_logf': jax.nn.log_sigmoid(head_bias + nrm((N_ODD, n_pool, PAGE_SIZE, C_HEADS), 0.5)),
        'page_table': page_table,
        'w_in_ab': nrm((N_EVEN, D_MODEL, PROJ_AB), inv),
        'mu_shift': uni((N_EVEN, RWKV_PROJ), 0.1, 0.9),
        'w0': uni((N_EVEN, A_WIDTH), -5.0, 0.5),
        'w_lora_w': nrm((N_EVEN, DECAY_LORA, A_WIDTH), 0.1 * DECAY_LORA ** -0.5),
        'a0': nrm((N_EVEN, A_WIDTH), 0.1),
        'w_lora_a': nrm((N_EVEN, AAA_LORA, A_WIDTH), 0.1 * AAA_LORA ** -0.5),
        'w_lora_g': nrm((N_EVEN, GATE_LORA, A_WIDTH), GATE_LORA ** -0.5),
        'k_k': 0.85 + nrm((N_EVEN, A_WIDTH), 0.05),
        'k_a': 1.0 + nrm((N_EVEN, A_WIDTH), 0.05),
        'r_k': nrm((N_EVEN, A_HEADS, A_HEAD), 0.1),
        'lnx_g': 1.0 + nrm((N_EVEN, A_WIDTH), 0.01),
        'lnx_b': nrm((N_EVEN, A_WIDTH), 0.01),
        's5_lambda_re': -0.5 + nrm((N_EVEN, S5_GROUPS, S5_STATE), 0.01),
        's5_lambda_im': math.pi * jnp.arange(S5_STATE, dtype=f32) + nrm((N_EVEN, S5_GROUPS, S5_STATE), 0.01),
        's5_log_dt': uni((N_EVEN, S5_GROUPS), math.log(0.001), math.log(0.1)),
        's5_b_re': nrm((N_EVEN, S5_GROUPS, S5_STATE, S5_GROUP), (2.0 * S5_GROUP) ** -0.5),
        's5_b_im': nrm((N_EVEN, S5_GROUPS, S5_STATE, S5_GROUP), (2.0 * S5_GROUP) ** -0.5),
        's5_c_re': nrm((N_EVEN, S5_GROUPS, S5_GROUP, S5_STATE), (2.0 * S5_STATE) ** -0.5),
        's5_c_im': nrm((N_EVEN, S5_GROUPS, S5_GROUP, S5_STATE), (2.0 * S5_STATE) ** -0.5),
        's5_d': nrm((N_EVEN, S5_GROUPS, S5_GROUP), 1.0),
        's5_w_glu': nrm((N_EVEN, B_WIDTH, B_WIDTH), B_WIDTH ** -0.5),
        's5_b_glu': nrm((N_EVEN, B_WIDTH), 0.01),
        'w_out_ab': nrm((N_EVEN, D_MODEL, D_MODEL), inv * BETA),
        'w_in_c': nrm((N_ODD, D_MODEL, PROJ_C), inv),
        'b_forget': head_bias + nrm((N_ODD, C_HEADS), 0.1),
        'w_out_c': nrm((N_ODD, D_MODEL, D_MODEL), inv * BETA),
        'ln1_g': 1.0 + nrm((DEPTH, D_MODEL), 0.01),
        'ln1_b': nrm((DEPTH, D_MODEL), 0.01),
        'ln2_g': 1.0 + nrm((DEPTH, D_MODEL), 0.01),
        'ln2_b': nrm((DEPTH, D_MODEL), 0.01),
        'ffn_w_gate': nrm((DEPTH, D_MODEL, D_FF), inv),
        'ffn_w_up': nrm((DEPTH, D_MODEL, D_FF), inv),
        'ffn_w_down': nrm((DEPTH, D_FF, D_MODEL), D_FF ** -0.5 * BETA),
    }


def reference(x_prompt, x_sample, state_shift, state_wkv, state_s5_re, state_s5_im, cache_k, cache_v,
              cache_logf, page_table, w_in_ab, mu_shift, w0, w_lora_w, a0, w_lora_a, w_lora_g, k_k, k_a,
              r_k, lnx_g, lnx_b, s5_lambda_re, s5_lambda_im, s5_log_dt, s5_b_re, s5_b_im, s5_c_re, s5_c_im,
              s5_d, s5_w_glu, s5_b_glu, w_out_ab, w_in_c, b_forget, w_out_c, ln1_g, ln1_b, ln2_g, ln2_b,
              ffn_w_gate, ffn_w_up, ffn_w_down):
    f32 = jnp.float32
    yp, ys = x_prompt, x_sample
    bp, bs = x_prompt.shape[0], x_sample.shape[0]
    shift_p, shift_s, wkv_p, wkv_s = [], [], [], []
    s5re_p, s5re_s, s5im_p, s5im_s = [], [], [], []
    k_p, k_s, v_p, v_s, lf_p, lf_s = [], [], [], [], [], []
    for layer in range(DEPTH):
        j = layer // 2
        if layer % 2 == 0:
            ab = (w_in_ab[j], mu_shift[j], w0[j], w_lora_w[j], a0[j], w_lora_a[j], w_lora_g[j], k_k[j], k_a[j],
                  r_k[j], lnx_g[j], lnx_b[j], s5_lambda_re[j], s5_lambda_im[j], s5_log_dt[j], s5_b_re[j],
                  s5_b_im[j], s5_c_re[j], s5_c_im[j], s5_d[j], s5_w_glu[j], s5_b_glu[j], w_out_ab[j])
            mp, sh, wk, hr, hi = mixer_ab(yp, jnp.zeros((bp, RWKV_PROJ), f32),
                                          jnp.zeros((bp, A_HEADS, A_HEAD, A_HEAD), f32),
                                          jnp.zeros((bp, S5_GROUPS, S5_STATE), f32),
                                          jnp.zeros((bp, S5_GROUPS, S5_STATE), f32), *ab)
            shift_p.append(sh); wkv_p.append(wk); s5re_p.append(hr); s5im_p.append(hi)
            ms, sh, wk, hr, hi = mixer_ab(ys, state_shift[j], state_wkv[j], state_s5_re[j], state_s5_im[j], *ab)
            shift_s.append(sh); wkv_s.append(wk); s5re_s.append(hr); s5im_s.append(hi)
        else:
            mp, kk_, vv_, lf_ = fox_prompt(yp, w_in_c[j], b_forget[j], w_out_c[j])
            k_p.append(kk_); v_p.append(vv_); lf_p.append(lf_)
            ms, kk_, vv_, lf_ = fox_sample(ys, cache_k, cache_v, cache_logf, j, page_table,
                                           w_in_c[j], b_forget[j], w_out_c[j])
            k_s.append(kk_); v_s.append(vv_); lf_s.append(lf_)
        yp = layer_norm(ALPHA * yp + mp.astype(yp.dtype), ln1_g[layer], ln1_b[layer])
        ys = layer_norm(ALPHA * ys + ms.astype(ys.dtype), ln1_g[layer], ln1_b[layer])
        yp = layer_norm(ALPHA * yp + swiglu(yp, ffn_w_gate[layer], ffn_w_up[layer], ffn_w_down[layer]),
                        ln2_g[layer], ln2_b[layer])
        ys = layer_norm(ALPHA * ys + swiglu(ys, ffn_w_gate[layer], ffn_w_up[layer], ffn_w_down[layer]),
                        ln2_g[layer], ln2_b[layer])
    return (yp, ys,
            jnp.stack(shift_p), jnp.stack(shift_s), jnp.stack(wkv_p), jnp.stack(wkv_s),
            jnp.stack(s5re_p), jnp.stack(s5re_s), jnp.stack(s5im_p), jnp.stack(s5im_s),
            jnp.stack(k_p), jnp.stack(k_s), jnp.stack(v_p), jnp.stack(v_s),
            jnp.stack(lf_p), jnp.stack(lf_s))
```

```python
import functools
import math

import jax
import jax.numpy as jnp
from jax import lax
from jax.experimental import pallas as pl
from jax.experimental.pallas import tpu as pltpu

F32, BF16 = jnp.float32, jnp.bfloat16
HI = lax.Precision.HIGHEST

LANES = 128
SUBLANES = 8
VMEM_LIMIT_BYTES = 56 * 1024 * 1024

HEAD = 64
S5_GROUP = 16
S5_STATE = 64
LORA_WA = 128
LORA_G = 128
GN_EPS = 64e-5
LN_EPS = 1e-5
RWKV_CHUNK = 64


def _dot(a, b, precision=None):
    return jnp.dot(a, b, preferred_element_type=F32, precision=precision)


def _dot_nt(a, b, precision=None):
    return lax.dot_general(a, b, (((1,), (1,)), ((), ())), preferred_element_type=F32, precision=precision)


def _dot_tn(a, b, precision=None):
    return lax.dot_general(a, b, (((0,), (0,)), ((), ())), preferred_element_type=F32, precision=precision)


def _params(*semantics):
    return pltpu.CompilerParams(dimension_semantics=semantics, vmem_limit_bytes=VMEM_LIMIT_BYTES)


def _row_tile(m, preferred):
    tm = min(m, preferred)
    assert m % tm == 0, (m, tm)
    return tm


def _layer_norm(z, g, b):
    mu = jnp.mean(z, axis=-1, keepdims=True)
    zc = z - mu
    var = jnp.mean(zc * zc, axis=-1, keepdims=True)
    return zc * lax.rsqrt(var + LN_EPS) * g + b


def _softplus(z):
    return jnp.maximum(z, 0.0) + jnp.log1p(jnp.exp(-jnp.abs(z)))


def _proj_kernel(x_ref, w_ref, *o_refs, widths):
    acc = _dot(x_ref[...].astype(BF16), w_ref[...])
    off = 0
    for o_ref, n in zip(o_refs, widths):
        o_ref[...] = acc[:, off:off + n].astype(o_ref.dtype)
        off += n


def _proj(x, w, widths, dtypes, tm):
    m, k = x.shape
    n = w.shape[1]
    assert sum(widths) == n
    return pl.pallas_call(
        functools.partial(_proj_kernel, widths=tuple(widths)),
        grid=(m // tm,),
        in_specs=[pl.BlockSpec((tm, k), lambda i: (i, 0)), pl.BlockSpec((k, n), lambda i: (0, 0))],
        out_specs=[pl.BlockSpec((tm, wd), lambda i: (i, 0)) for wd in widths],
        out_shape=[jax.ShapeDtypeStruct((m, wd), dt) for wd, dt in zip(widths, dtypes)],
        compiler_params=_params("parallel"),
        name="proj_split",
    )(x, w)


def _outproj_ln_kernel(*refs, n_parts, alpha):
    part_refs = refs[:n_parts]
    w_ref, x_ref, g_ref, b_ref, o_ref = refs[n_parts:]
    off = 0
    mix = None
    for p_ref in part_refs:
        kp = p_ref.shape[1]
        t = _dot(p_ref[...].astype(BF16), w_ref[off:off + kp, :])
        mix = t if mix is None else mix + t
        off += kp
    o_ref[...] = _layer_norm(alpha * x_ref[...] + mix, g_ref[...], b_ref[...])


def _outproj_ln(parts, w, x, g, b, alpha, tm):
    m, d = x.shape
    in_specs = [pl.BlockSpec((tm, p.shape[1]), lambda i: (i, 0)) for p in parts]
    in_specs += [pl.BlockSpec(w.shape, lambda i: (0, 0)), pl.BlockSpec((tm, d), lambda i: (i, 0)),
                 pl.BlockSpec((1, d), lambda i: (0, 0)), pl.BlockSpec((1, d), lambda i: (0, 0))]
    return pl.pallas_call(
        functools.partial(_outproj_ln_kernel, n_parts=len(parts), alpha=alpha),
        grid=(m // tm,),
        in_specs=in_specs,
        out_specs=pl.BlockSpec((tm, d), lambda i: (i, 0)),
        out_shape=jax.ShapeDtypeStruct((m, d), F32),
        compiler_params=_params("parallel"),
        name="outproj_ln",
    )(*parts, w, x, g, b)


def _ffn_chunk(dff):
    for c in (512, 256, 128):
        if dff % c == 0:
            return c
    return dff


def _ffn_ln_kernel(x_ref, wg_ref, wu_ref, wd_ref, g_ref, b_ref, o_ref, *, alpha, chunk):
    x = x_ref[...]
    xb = x.astype(BF16)
    dff = wg_ref.shape[1]
    out = None
    for c in range(dff // chunk):
        cs = slice(c * chunk, (c + 1) * chunk)
        gate = _dot(xb, wg_ref[:, cs])
        up = _dot(xb, wu_ref[:, cs])
        h = (gate * jax.nn.sigmoid(gate) * up).astype(BF16)
        t = _dot(h, wd_ref[cs, :])
        out = t if out is None else out + t
    o_ref[...] = _layer_norm(alpha * x + out, g_ref[...], b_ref[...])


def _ffn_ln(x, wg, wu, wd, g, b, alpha, tm):
    m, d = x.shape
    dff = wg.shape[1]
    const = lambda i: (0, 0)
    return pl.pallas_call(
        functools.partial(_ffn_ln_kernel, alpha=alpha, chunk=_ffn_chunk(dff)),
        grid=(m // tm,),
        in_specs=[pl.BlockSpec((tm, d), lambda i: (i, 0)), pl.BlockSpec((d, dff), const),
                  pl.BlockSpec((d, dff), const), pl.BlockSpec((dff, d), const),
                  pl.BlockSpec((1, d), const), pl.BlockSpec((1, d), const)],
        out_specs=pl.BlockSpec((tm, d), lambda i: (i, 0)),
        out_shape=jax.ShapeDtypeStruct((m, d), F32),
        compiler_params=_params("parallel"),
        name="ffn_ln",
    )(x, wg, wu, wd, g, b)


def _rwkv_kernel(p_ref, sprev_ref, s0_ref, mu_ref, w0_ref, wlw_ref, a0_ref, wla_ref, wlg_ref, kk_ref, ka_ref,
                 rk_ref, gng_ref, gnb_ref, ones_ref, y_ref, shift_ref, sout_ref, carry_sc, state_sc,
                 *, chunk, t_total, n_chunks):
    c = pl.program_id(1)
    d = y_ref.shape[-1]
    n_pairs = d // LANES
    ell = chunk

    @pl.when(c == 0)
    def _():
        carry_sc[...] = sprev_ref[...]
        state_sc[...] = s0_ref[...]

    p = p_ref[...]
    row = lax.broadcasted_iota(jnp.int32, (ell, 1), 0)
    prev = jnp.where(row == 0, carry_sc[...], pltpu.roll(p, 1, 0))
    ps = p + (prev - p) * mu_ref[...]
    carry_sc[...] = p[ell - 1:ell, :]

    r = ps[:, 0:d]
    k = ps[:, d:2 * d]
    v = ps[:, 2 * d:3 * d]
    x_wa = ps[:, 3 * d:3 * d + LORA_WA]
    x_g = ps[:, 3 * d + LORA_WA:3 * d + LORA_WA + LORA_G]
    ones_blk = ones_ref[...]

    w_log = -_softplus(-(w0_ref[...] + _dot(jnp.tanh(x_wa), wlw_ref[...], HI))) - 0.5
    logw = -jnp.exp(w_log)
    a = jax.nn.sigmoid(a0_ref[...] + _dot(x_wa, wla_ref[...], HI))
    g = _dot(jax.nn.sigmoid(x_g), wlg_ref[...], HI)
    kk = k * kk_ref[...]
    kk = kk / jnp.maximum(jnp.sqrt(_dot(kk * kk, ones_blk, HI)), 1e-12)
    k2 = k * (1.0 + (a - 1.0) * ka_ref[...])
    bv = kk * a
    if t_total % ell:
        valid = (c * ell + row) < t_total
        logw = jnp.where(valid, logw, 0.0)
        kk = jnp.where(valid, kk, 0.0)
        bv = jnp.where(valid, bv, 0.0)
        k2 = jnp.where(valid, k2, 0.0)

    ti = lax.broadcasted_iota(jnp.int32, (ell, ell), 0)
    tj = lax.broadcasted_iota(jnp.int32, (ell, ell), 1)
    cl = _dot((ti >= tj).astype(F32), logw, HI)
    e_in = jnp.exp(cl)
    e_neg = jnp.exp(-cl)
    a_t = -kk * jnp.exp(cl - logw)
    r_t = r * e_in
    b_t = bv * e_neg
    k_t = k2 * e_neg
    w_end = e_in[ell - 1:ell, :]

    lane = lax.broadcasted_iota(jnp.int32, (1, LANES), 1)
    first = lane < HEAD
    si = lax.broadcasted_iota(jnp.int32, (2 * ell, 2 * ell), 0) % ell
    sj = lax.broadcasted_iota(jnp.int32, (2 * ell, 2 * ell), 1) % ell
    strict = si > sj
    incl = si >= sj
    eye = (lax.broadcasted_iota(jnp.int32, (2 * ell, 2 * ell), 0)
           == lax.broadcasted_iota(jnp.int32, (2 * ell, 2 * ell), 1)).astype(F32)

    def stack(z):
        return jnp.concatenate([jnp.where(first, z, 0.0), jnp.where(first, 0.0, z)], axis=0)

    y_pairs = []
    for pr in range(n_pairs):
        sl = slice(pr * LANES, (pr + 1) * LANES)
        a_s, r_s, b_s, k_s, v_s = stack(a_t[:, sl]), stack(r_t[:, sl]), stack(b_t[:, sl]), stack(k_t[:, sl]), stack(v[:, sl])
        state = state_sc[pr]
        gram = _dot_nt(jnp.concatenate([a_s, r_s], axis=0), jnp.concatenate([b_s, k_s], axis=0), HI)
        n_ab = jnp.where(strict, gram[:2 * ell, :2 * ell], 0.0)
        n_ak = jnp.where(strict, gram[:2 * ell, 2 * ell:], 0.0)
        n_rb = jnp.where(incl, gram[2 * ell:, :2 * ell], 0.0)
        n_rk = jnp.where(incl, gram[2 * ell:, 2 * ell:], 0.0)
        inv = eye + n_ab
        pw = n_ab
        for _ in range(int(math.log2(ell)) - 1):
            pw = _dot(pw, pw, HI)
            inv = inv + _dot(pw, inv, HI)
        u = _dot(inv, _dot_nt(a_s, state, HI) + _dot(n_ak, v_s, HI), HI)
        y2 = _dot_nt(r_s, state, HI) + _dot(n_rb, u, HI) + _dot(n_rk, v_s, HI)
        y_pairs.append(y2[:ell] + y2[ell:])
        upd = _dot_tn(jnp.concatenate([u, v_s], axis=0), jnp.concatenate([b_s, k_s], axis=0), HI)
        state_sc[pr] = (state + upd) * w_end[:, sl]

    y = jnp.concatenate(y_pairs, axis=1)
    inv_head = 1.0 / HEAD
    mean = _dot(y, ones_blk, HI) * inv_head
    yc = y - mean
    var = _dot(yc * yc, ones_blk, HI) * inv_head
    yn = yc * lax.rsqrt(var + GN_EPS) * gng_ref[...] + gnb_ref[...]
    bonus = _dot(r * k2 * rk_ref[...], ones_blk, HI) * v
    y_ref[...] = ((yn + bonus) * g).astype(y_ref.dtype)

    @pl.when(c == n_chunks - 1)
    def _():
        last = (t_total - 1) % ell
        shift_ref[...] = p[last:last + 1, :]
        sout_ref[...] = state_sc[...]


def _pad_rows(w, rows, offset):
    out = jnp.zeros((rows, w.shape[1]), F32)
    return out.at[offset:offset + w.shape[0]].set(w.astype(F32))


def _rwkv(p, shift_prev, wkv0, lw):
    bsz, t_total, width = p.shape
    d = width - LORA_WA - LORA_G
    d //= 3
    n_pairs = d // LANES
    ell = RWKV_CHUNK
    n_chunks = -(-t_total // ell)
    t_pad = n_chunks * ell
    if t_pad != t_total:
        p = jnp.pad(p, ((0, 0), (0, t_pad - t_total), (0, 0)))
    s0 = wkv0.astype(F32).reshape(bsz, n_pairs, 2, HEAD, HEAD)
    zero = jnp.zeros_like(s0[:, :, 0])
    s0 = jnp.concatenate([jnp.concatenate([s0[:, :, 0], zero], axis=-1),
                          jnp.concatenate([zero, s0[:, :, 1]], axis=-1)], axis=-2)
    row = lambda z: z.astype(F32).reshape(1, -1)
    ones_blk = jnp.kron(jnp.eye(d // HEAD, dtype=F32), jnp.ones((HEAD, HEAD), F32))
    consts = [row(lw["mu"]), row(lw["w0"]), _pad_rows(lw["w_lora_w"], LORA_WA, 0), row(lw["a0"]),
              _pad_rows(lw["w_lora_a"], LORA_WA, LORA_WA // 2), lw["w_lora_g"].astype(F32), row(lw["k_k"]),
              row(lw["k_a"]), row(lw["r_k"]), row(lw["lnx_g"]), row(lw["lnx_b"]), ones_blk]
    const_specs = [pl.BlockSpec(cst.shape, lambda b, c: (0, 0)) for cst in consts]
    y, shift, sout = pl.pallas_call(
        functools.partial(_rwkv_kernel, chunk=ell, t_total=t_total, n_chunks=n_chunks),
        grid=(bsz, n_chunks),
        in_specs=[pl.BlockSpec((None, ell, width), lambda b, c: (b, c, 0)),
                  pl.BlockSpec((None, 1, width), lambda b, c: (b, 0, 0)),
                  pl.BlockSpec((None, n_pairs, LANES, LANES), lambda b, c: (b, 0, 0, 0))] + const_specs,
        out_specs=[pl.BlockSpec((None, ell, d), lambda b, c: (b, c, 0)),
                   pl.BlockSpec((None, 1, width), lambda b, c: (b, 0, 0)),
                   pl.BlockSpec((None, n_pairs, LANES, LANES), lambda b, c: (b, 0, 0, 0))],
        out_shape=[jax.ShapeDtypeStruct((bsz, t_pad, d), BF16),
                   jax.ShapeDtypeStruct((bsz, 1, width), F32),
                   jax.ShapeDtypeStruct((bsz, n_pairs, LANES, LANES), F32)],
        scratch_shapes=[pltpu.VMEM((1, width), F32), pltpu.VMEM((n_pairs, LANES, LANES), F32)],
        compiler_params=_params("parallel", "arbitrary"),
        name="rwkv7_chunked",
    )(p, shift_prev.astype(F32).reshape(bsz, 1, width), s0, *consts)
    wkv = jnp.stack([sout[:, :, :HEAD, :HEAD], sout[:, :, HEAD:, HEAD:]], axis=2)
    return y[:, :t_total], shift[:, 0], wkv.reshape(bsz, 2 * n_pairs, HEAD, HEAD)


def _gelu_tanh(x):
    return 0.5 * x * (1.0 + jnp.tanh(math.sqrt(2.0 / math.pi) * (x + 0.044715 * (x * x * x))))


def _s5_kernel(u_ref, h0re_ref, h0im_ref, abre_ref, abim_ref, wbre_ref, wbim_ref, wcre_ref, wcim_ref, dskip_ref,
               wglu_ref, bglu_ref, y_ref, hre_ref, him_ref, ere_sc, eim_sc, hre_sc, him_sc, *, tb, n_blocks, blocks_per_group):
    tblk = pl.program_id(1)
    nb = SUBLANES
    n_state = hre_sc.shape[1]

    @pl.when(tblk == 0)
    def _():
        hre_sc[...] = h0re_ref[...]
        him_sc[...] = h0im_ref[...]

    if tb == 1:
        u = u_ref[:, 0, :]
    else:
        u = u_ref[...].reshape(nb * tb, u_ref.shape[-1])
    ub = u.astype(BF16)
    n_blk = n_state // LANES
    e_re = _dot(ub, wbre_ref[...])
    e_im = _dot(ub, wbim_ref[...])
    for jb in range(n_blk):
        ere_sc[jb] = e_re[:, jb * LANES:(jb + 1) * LANES]
        eim_sc[jb] = e_im[:, jb * LANES:(jb + 1) * LANES]

    rows_at = (lambda t: pl.ds(t, nb, stride=tb)) if tb > 1 else (lambda t: pl.ds(0, nb))
    for cg in range(n_blk // blocks_per_group):
        blks = range(cg * blocks_per_group, (cg + 1) * blocks_per_group)
        a_re = [jnp.broadcast_to(abre_ref[:, jb * LANES:(jb + 1) * LANES], (nb, LANES)) for jb in blks]
        a_im = [jnp.broadcast_to(abim_ref[:, jb * LANES:(jb + 1) * LANES], (nb, LANES)) for jb in blks]

        def step(t, carry, blks=blks, a_re=a_re, a_im=a_im):
            rows = rows_at(t)
            new_re, new_im = [], []
            for n, jb in enumerate(blks):
                h_re, h_im = carry[0][n], carry[1][n]
                n_re = a_re[n] * h_re - a_im[n] * h_im + ere_sc[jb, rows, :]
                n_im = a_re[n] * h_im + a_im[n] * h_re + eim_sc[jb, rows, :]
                ere_sc[jb, rows, :] = n_re
                eim_sc[jb, rows, :] = n_im
                new_re.append(n_re)
                new_im.append(n_im)
            return tuple(new_re), tuple(new_im)

        init = (tuple(hre_sc[:, jb * LANES:(jb + 1) * LANES] for jb in blks),
                tuple(him_sc[:, jb * LANES:(jb + 1) * LANES] for jb in blks))
        fin_re, fin_im = lax.fori_loop(0, tb, step, init)
        for n, jb in enumerate(blks):
            hre_sc[:, jb * LANES:(jb + 1) * LANES] = fin_re[n]
            him_sc[:, jb * LANES:(jb + 1) * LANES] = fin_im[n]

    h_re_all = jnp.concatenate([ere_sc[jb] for jb in range(n_blk)], axis=1)
    h_im_all = jnp.concatenate([eim_sc[jb] for jb in range(n_blk)], axis=1)
    y = (_dot(h_re_all.astype(BF16), wcre_ref[...]) - _dot(h_im_all.astype(BF16), wcim_ref[...])
         + dskip_ref[...] * u)
    y = _gelu_tanh(y)
    y = y * jax.nn.sigmoid(_dot(y.astype(BF16), wglu_ref[...]) + bglu_ref[...])
    if tb == 1:
        y_ref[:, 0, :] = y.astype(y_ref.dtype)
    else:
        y_ref[...] = y.reshape(nb, tb, y.shape[-1]).astype(y_ref.dtype)

    @pl.when(tblk == n_blocks - 1)
    def _():
        hre_ref[...] = hre_sc[...]
        him_ref[...] = him_sc[...]


def _s5_weights(lw):
    lr, li = lw["s5_lambda_re"].astype(F32), lw["s5_lambda_im"].astype(F32)
    n_groups = lr.shape[0]
    dt = jnp.exp(lw["s5_log_dt"].astype(F32))[:, None]
    mag = jnp.exp(lr * dt)
    ab_re, ab_im = mag * jnp.cos(li * dt), mag * jnp.sin(li * dt)
    den = lr * lr + li * li
    n_re = ab_re - 1.0
    f_re = (n_re * lr + ab_im * li) / den
    f_im = (ab_im * lr - n_re * li) / den
    br, bi = lw["s5_b_re"].astype(F32), lw["s5_b_im"].astype(F32)
    bb_re = f_re[..., None] * br - f_im[..., None] * bi
    bb_im = f_re[..., None] * bi + f_im[..., None] * br
    eye = jnp.eye(n_groups, dtype=F32)
    n_in, n_state = n_groups * S5_GROUP, n_groups * S5_STATE
    expand_b = lambda z: jnp.einsum("gpc,gh->gchp", z, eye).reshape(n_in, n_state).astype(BF16)
    expand_c = lambda z: jnp.einsum("gcp,gh->gphc", z.astype(F32), eye).reshape(n_state, n_in).astype(BF16)
    return dict(ab_re=ab_re.reshape(1, n_state), ab_im=ab_im.reshape(1, n_state),
                wb_re=expand_b(bb_re), wb_im=expand_b(bb_im),
                wc_re=expand_c(lw["s5_c_re"]), wc_im=expand_c(lw["s5_c_im"]),
                d_skip=lw["s5_d"].astype(F32).reshape(1, n_in),
                w_glu=lw["s5_w_glu"].astype(BF16), b_glu=lw["s5_b_glu"].astype(F32).reshape(1, n_in))


def _s5(u, h_re0, h_im0, sw, tb):
    bsz, t_total, n_in = u.shape
    n_state = sw["ab_re"].shape[1]
    nb = SUBLANES
    assert bsz % nb == 0 and t_total % tb == 0
    n_blocks = t_total // tb
    n_blk = n_state // LANES
    blocks_per_group = math.gcd(n_blk, 4)
    const = lambda g, t: (0, 0)
    consts = [sw["ab_re"], sw["ab_im"], sw["wb_re"], sw["wb_im"], sw["wc_re"], sw["wc_im"], sw["d_skip"],
              sw["w_glu"], sw["b_glu"]]
    y, h_re, h_im = pl.pallas_call(
        functools.partial(_s5_kernel, tb=tb, n_blocks=n_blocks, blocks_per_group=blocks_per_group),
        grid=(bsz // nb, n_blocks),
        in_specs=[pl.BlockSpec((nb, tb, n_in), lambda g, t: (g, t, 0)),
                  pl.BlockSpec((nb, n_state), lambda g, t: (g, 0)),
                  pl.BlockSpec((nb, n_state), lambda g, t: (g, 0))]
                 + [pl.BlockSpec(cst.shape, const) for cst in consts],
        out_specs=[pl.BlockSpec((nb, tb, n_in), lambda g, t: (g, t, 0)),
                   pl.BlockSpec((nb, n_state), lambda g, t: (g, 0)),
                   pl.BlockSpec((nb, n_state), lambda g, t: (g, 0))],
        out_shape=[jax.ShapeDtypeStruct((bsz, t_total, n_in), BF16),
                   jax.ShapeDtypeStruct((bsz, n_state), F32),
                   jax.ShapeDtypeStruct((bsz, n_state), F32)],
        scratch_shapes=[pltpu.VMEM((n_blk, nb * tb, LANES), F32), pltpu.VMEM((n_blk, nb * tb, LANES), F32),
                        pltpu.VMEM((nb, n_state), F32), pltpu.VMEM((nb, n_state), F32)],
        compiler_params=_params("parallel", "arbitrary"),
        name="s5_scan",
    )(u, h_re0.astype(F32).reshape(bsz, n_state), h_im0.astype(F32).reshape(bsz, n_state), *consts)
    shape = (bsz, n_state // S5_STATE, S5_STATE)
    return y, h_re.reshape(shape), h_im.reshape(shape)


def _log_sigmoid(z):
    return jnp.minimum(z, 0.0) - jnp.log1p(jnp.exp(-jnp.abs(z)))


def _fox_proj_kernel(x_ref, w_ref, wf_ref, bf_ref, q_ref, k_ref, v_ref, kb_ref, vb_ref, lf_ref, c_ref, carry_sc,
                     *, d, scale, tiles_per_seq):
    i = pl.program_id(0)
    tm = x_ref.shape[0]
    xb = x_ref[...].astype(BF16)
    acc = _dot(xb, w_ref[...])
    q_ref[...] = (acc[:, :d] * scale).astype(BF16)
    k = acc[:, d:2 * d]
    v = acc[:, 2 * d:3 * d]
    k_ref[...] = k
    v_ref[...] = v
    kb_ref[...] = k.astype(BF16)
    vb_ref[...] = v.astype(BF16)
    lf = _log_sigmoid(_dot(xb, wf_ref[...]) + bf_ref[...])
    lf_ref[...] = lf

    @pl.when(i % tiles_per_seq == 0)
    def _():
        carry_sc[...] = jnp.zeros_like(carry_sc)

    ti = lax.broadcasted_iota(jnp.int32, (tm, tm), 0)
    tj = lax.broadcasted_iota(jnp.int32, (tm, tm), 1)
    c = _dot((ti >= tj).astype(F32), lf, HI) + carry_sc[...]
    c_ref[...] = c
    carry_sc[...] = c[tm - 1:tm, :]


def _fox_proj(x, w_qkv, w_f, b_f, seq_len, tm):
    m, kdim = x.shape
    d = w_qkv.shape[1] // 3
    nh = w_f.shape[1]
    assert seq_len % tm == 0 or tm % seq_len == 0
    row = lambda n: pl.BlockSpec((tm, n), lambda i: (i, 0))
    return pl.pallas_call(
        functools.partial(_fox_proj_kernel, d=d, scale=HEAD ** -0.5, tiles_per_seq=max(seq_len // tm, 1)),
        grid=(m // tm,),
        in_specs=[row(kdim), pl.BlockSpec(w_qkv.shape, lambda i: (0, 0)), pl.BlockSpec(w_f.shape, lambda i: (0, 0)),
                  pl.BlockSpec((1, nh), lambda i: (0, 0))],
        out_specs=[row(d), row(d), row(d), row(d), row(d), row(nh), row(nh)],
        out_shape=[jax.ShapeDtypeStruct((m, d), BF16), jax.ShapeDtypeStruct((m, d), F32),
                   jax.ShapeDtypeStruct((m, d), F32), jax.ShapeDtypeStruct((m, d), BF16),
                   jax.ShapeDtypeStruct((m, d), BF16), jax.ShapeDtypeStruct((m, nh), F32),
                   jax.ShapeDtypeStruct((m, nh), F32)],
        scratch_shapes=[pltpu.VMEM((1, nh), F32)],
        compiler_params=_params("arbitrary"),
        name="fox_proj",
    )(x, w_qkv, w_f, b_f)


def _fox_attn_kernel(q_ref, k_ref, v_ref, c_ref, o_ref, m_sc, l_sc, acc_sc, *, tq):
    qi = pl.program_id(2)
    lane = lax.broadcasted_iota(jnp.int32, (1, LANES), 1)
    q = q_ref[...]
    ri = lax.broadcasted_iota(jnp.int32, (tq, tq), 0)
    ci = lax.broadcasted_iota(jnp.int32, (tq, tq), 1)
    causal = ci <= ri
    outs = []
    for h in range(LANES // HEAD):
        in_head = (lane // HEAD) == h
        qm = jnp.where(in_head, q, jnp.zeros_like(q))
        m_sc[...] = jnp.full_like(m_sc, -jnp.inf)
        l_sc[...] = jnp.zeros_like(l_sc)
        acc_sc[...] = jnp.zeros_like(acc_sc)

        def tile(ki, masked, h=h, qm=qm):
            start = pl.multiple_of(ki * tq, tq)
            kt = k_ref[pl.ds(start, tq), :]
            vt = v_ref[pl.ds(start, tq), :]
            s = _dot_nt(qm, kt) - c_ref[h, :, pl.ds(start, tq)]
            if masked:
                s = jnp.where(causal, s, -jnp.inf)
            m_prev = m_sc[...]
            m_new = jnp.maximum(m_prev, jnp.max(s, axis=1, keepdims=True))
            alpha = jnp.exp(m_prev - m_new)
            p = jnp.exp(s - m_new)
            l_sc[...] = alpha * l_sc[...] + jnp.sum(p, axis=1, keepdims=True)
            acc_sc[...] = alpha * acc_sc[...] + _dot(p.astype(BF16), vt)
            m_sc[...] = m_new

        def body(ki, carry):
            tile(ki, False)
            return carry

        lax.fori_loop(0, qi, body, 0)
        tile(qi, True)
        outs.append(acc_sc[...] / l_sc[...])
    o_ref[...] = jnp.where(lane < HEAD, outs[0], outs[1]).astype(o_ref.dtype)


def _fox_attn(q, kb, vb, c, tq):
    bsz, t_total, d = q.shape
    n_pairs = d // LANES
    hp = LANES // HEAD
    return pl.pallas_call(
        functools.partial(_fox_attn_kernel, tq=tq),
        grid=(bsz, n_pairs, t_total // tq),
        in_specs=[pl.BlockSpec((None, tq, LANES), lambda b, h, i: (b, i, h)),
                  pl.BlockSpec((None, t_total, LANES), lambda b, h, i: (b, 0, h)),
                  pl.BlockSpec((None, t_total, LANES), lambda b, h, i: (b, 0, h)),
                  pl.BlockSpec((None, hp, 1, t_total), lambda b, h, i: (b, h, 0, 0))],
        out_specs=pl.BlockSpec((None, tq, LANES), lambda b, h, i: (b, i, h)),
        out_shape=jax.ShapeDtypeStruct((bsz, t_total, d), BF16),
        scratch_shapes=[pltpu.VMEM((tq, 1), F32), pltpu.VMEM((tq, 1), F32), pltpu.VMEM((tq, LANES), F32)],
        compiler_params=_params("parallel", "parallel", "arbitrary"),
        name="fox_attn",
    )(q, kb, vb, c)


def _fox_decode_kernel(pt_ref, qh_ref, knew_ref, vnew_ref, lfnew_ref, *refs, pages_per_step, n_steps):
    del pt_ref
    npp = pages_per_step
    k_refs, v_refs, lf_refs = refs[:npp], refs[npp:2 * npp], refs[2 * npp:3 * npp]
    o_ref, m_sc, l_sc, acc_sc, suf_sc = refs[3 * npp:]
    j = pl.program_id(1)

    @pl.when(j == 0)
    def _():
        m_sc[...] = jnp.full_like(m_sc, -jnp.inf)
        l_sc[...] = jnp.zeros_like(l_sc)
        acc_sc[...] = jnp.zeros_like(acc_sc)
        suf_sc[...] = jnp.zeros_like(suf_sc)

    qh = qh_ref[...]
    qb = qh.astype(BF16)
    c_new = lfnew_ref[...]
    page = k_refs[0].shape[0]
    later = (lax.broadcasted_iota(jnp.int32, (page, page), 0)
             > lax.broadcasted_iota(jnp.int32, (page, page), 1)).astype(F32)
    for i in range(npp):
        kp = k_refs[i][...].astype(BF16)
        vp = v_refs[i][...].astype(BF16)
        lft = lf_refs[i][...]
        s = _dot_nt(qb, kp) + c_new + _dot(lft, later, HI) + suf_sc[...]
        m_prev = m_sc[...]
        m_new = jnp.maximum(m_prev, jnp.max(s, axis=1, keepdims=True))
        alpha = jnp.exp(m_prev - m_new)
        p = jnp.exp(s - m_new)
        l_sc[...] = alpha * l_sc[...] + jnp.sum(p, axis=1, keepdims=True)
        acc_sc[...] = alpha * acc_sc[...] + _dot(p.astype(BF16), vp)
        m_sc[...] = m_new
        suf_sc[...] = suf_sc[...] + jnp.sum(lft, axis=1, keepdims=True)

    @pl.when(j == n_steps - 1)
    def _():
        s_new = jnp.sum(qh * knew_ref[...], axis=1, keepdims=True)
        m_prev = m_sc[...]
        m_new = jnp.maximum(m_prev, s_new)
        alpha = jnp.exp(m_prev - m_new)
        p_new = jnp.exp(s_new - m_new)
        l_fin = alpha * l_sc[...] + p_new
        o_heads = (alpha * acc_sc[...] + p_new * vnew_ref[...]) / l_fin
        nh, d = o_heads.shape
        own = (lax.broadcasted_iota(jnp.int32, (nh, d), 1) // HEAD) == lax.broadcasted_iota(jnp.int32, (nh, d), 0)
        o_ref[...] = jnp.sum(jnp.where(own, o_heads, 0.0), axis=0, keepdims=True)


def _fox_decode(q, k_new, v_new, lf_new, pool_k, pool_v, pool_lft, page_table, layer_offset, pages_per_step):
    bsz, d = q.shape
    nh = d // HEAD
    n_pages = page_table.shape[1]
    page = pool_k.shape[1]
    npp = pages_per_step
    assert n_pages % npp == 0
    n_steps = n_pages // npp
    own = (jnp.arange(d)[None, :] // HEAD) == jnp.arange(nh)[:, None]
    qh = jnp.where(own[None], q[:, None, :], 0.0)
    pt = page_table.reshape(-1).astype(jnp.int32)

    def page_map(i):
        return lambda b, j, pt_ref: (pt_ref[b * n_pages + (n_pages - 1 - (j * npp + i))] + layer_offset, 0, 0)

    per_seq = lambda n0, n1: pl.BlockSpec((None, n0, n1), lambda b, j, pt_ref: (b, 0, 0))
    in_specs = [per_seq(nh, d), per_seq(1, d), per_seq(1, d), per_seq(nh, 1)]
    in_specs += [pl.BlockSpec((None, page, d), page_map(i)) for i in range(npp)]
    in_specs += [pl.BlockSpec((None, page, d), page_map(i)) for i in range(npp)]
    in_specs += [pl.BlockSpec((None, nh, page), page_map(i)) for i in range(npp)]
    out = pl.pallas_call(
        functools.partial(_fox_decode_kernel, pages_per_step=npp, n_steps=n_steps),
        grid_spec=pltpu.PrefetchScalarGridSpec(
            num_scalar_prefetch=1,
            grid=(bsz, n_steps),
            in_specs=in_specs,
            out_specs=per_seq(1, d),
            scratch_shapes=[pltpu.VMEM((nh, 1), F32), pltpu.VMEM((nh, 1), F32), pltpu.VMEM((nh, d), F32),
                            pltpu.VMEM((nh, 1), F32)]),
        out_shape=jax.ShapeDtypeStruct((bsz, 1, d), F32),
        compiler_params=_params("parallel", "arbitrary"),
        name="fox_decode",
    )(pt, qh, k_new.reshape(bsz, 1, d), v_new.reshape(bsz, 1, d), lf_new.reshape(bsz, nh, 1),
      *([pool_k] * npp), *([pool_v] * npp), *([pool_lft] * npp))
    return out[:, 0]


def _mixer_ab(x2d, bsz, seq, shift_prev, wkv0, h_re0, h_im0, lw, tm, s5_tb):
    d_rwkv = lw["mu"].shape[0]
    d_s5 = lw["w_in"].shape[1] - d_rwkv
    p, u = _proj(x2d, lw["w_in"], (d_rwkv, d_s5), (F32, F32), tm)
    y_a, shift, wkv = _rwkv(p.reshape(bsz, seq, d_rwkv), shift_prev, wkv0, lw)
    y_b, h_re, h_im = _s5(u.reshape(bsz, seq, d_s5), h_re0, h_im0, lw["s5"], s5_tb)
    return (y_a.reshape(bsz * seq, -1), y_b.reshape(bsz * seq, -1)), shift, wkv, h_re, h_im


def kernel(x_prompt, x_sample, state_shift, state_wkv, state_s5_re, state_s5_im, cache_k, cache_v, cache_logf, page_table, w_in_ab, mu_shift, w0, w_lora_w, a0, w_lora_a, w_lora_g, k_k, k_a, r_k, lnx_g, lnx_b, s5_lambda_re, s5_lambda_im, s5_log_dt, s5_b_re, s5_b_im, s5_c_re, s5_c_im, s5_d, s5_w_glu, s5_b_glu, w_out_ab, w_in_c, b_forget, w_out_c, ln1_g, ln1_b, ln2_g, ln2_b, ffn_w_gate, ffn_w_up, ffn_w_down):
    bp, seq_p, d_model = x_prompt.shape
    bs, seq_s, _ = x_sample.shape
    assert seq_s == 1
    depth = ln1_g.shape[0]
    alpha = (2.0 * depth) ** 0.25
    n_heads = d_model // HEAD
    mp, ms = bp * seq_p, bs * seq_s
    tm_p, tm_s = _row_tile(mp, 512), _row_tile(ms, 512)
    tm_ffn = _row_tile(mp, 256)
    row = lambda z: z.astype(F32).reshape(1, -1)

    yp = x_prompt.reshape(mp, d_model)
    ys = x_sample.reshape(ms, d_model)
    n_pool, page = cache_k.shape[1], cache_k.shape[2]
    pool_k = cache_k.reshape(-1, page, d_model)
    pool_v = cache_v.reshape(-1, page, d_model)
    pool_lft = jnp.swapaxes(cache_logf.reshape(-1, page, n_heads), 1, 2)

    shift_p, shift_s, wkv_p, wkv_s = [], [], [], []
    s5re_p, s5re_s, s5im_p, s5im_s = [], [], [], []
    k_p, k_s, v_p, v_s, lf_p, lf_s = [], [], [], [], [], []
    for layer in range(depth):
        j = layer // 2
        if layer % 2 == 0:
            lw = dict(w_in=w_in_ab[j].astype(BF16), mu=mu_shift[j], w0=w0[j], w_lora_w=w_lora_w[j], a0=a0[j],
                      w_lora_a=w_lora_a[j], w_lora_g=w_lora_g[j], k_k=k_k[j], k_a=k_a[j], r_k=r_k[j],
                      lnx_g=lnx_g[j], lnx_b=lnx_b[j], s5_lambda_re=s5_lambda_re[j], s5_lambda_im=s5_lambda_im[j],
                      s5_log_dt=s5_log_dt[j], s5_b_re=s5_b_re[j], s5_b_im=s5_b_im[j], s5_c_re=s5_c_re[j],
                      s5_c_im=s5_c_im[j], s5_d=s5_d[j], s5_w_glu=s5_w_glu[j], s5_b_glu=s5_b_glu[j])
            lw["s5"] = _s5_weights(lw)
            w_out = w_out_ab[j].astype(BF16)
            n_s5 = lw["s5"]["ab_re"].shape[1]
            zeros = lambda *shape: jnp.zeros(shape, F32)
            parts_p, sh, wk, hr, hi = _mixer_ab(
                yp, bp, seq_p, zeros(bp, mu_shift.shape[1]), zeros(bp, (w0.shape[1]) // HEAD, HEAD, HEAD),
                zeros(bp, n_s5), zeros(bp, n_s5), lw, tm_p, min(seq_p, 64))
            shift_p.append(sh); wkv_p.append(wk); s5re_p.append(hr); s5im_p.append(hi)
            parts_s, sh, wk, hr, hi = _mixer_ab(
                ys, bs, seq_s, state_shift[j], state_wkv[j], state_s5_re[j], state_s5_im[j], lw, tm_s, 1)
            shift_s.append(sh); wkv_s.append(wk); s5re_s.append(hr); s5im_s.append(hi)
        else:
            w_c = w_in_c[j].astype(BF16)
            w_qkv, w_f = w_c[:, :3 * d_model], w_c[:, 3 * d_model:]
            b_f = row(b_forget[j])
            w_out = w_out_c[j].astype(BF16)
            q, k, v, kb, vb, lf, c = _fox_proj(yp, w_qkv, w_f, b_f, seq_p, tm_p)
            c = jnp.swapaxes(c.reshape(bp, seq_p, n_heads), 1, 2).reshape(bp, n_heads, 1, seq_p)
            o = _fox_attn(q.reshape(bp, seq_p, d_model), kb.reshape(bp, seq_p, d_model),
                          vb.reshape(bp, seq_p, d_model), c, min(seq_p, 512))
            parts_p = (o.reshape(mp, d_model),)
            k_p.append(k.reshape(bp, seq_p, n_heads, HEAD)); v_p.append(v.reshape(bp, seq_p, n_heads, HEAD))
            lf_p.append(lf.reshape(bp, seq_p, n_heads))
            q, k, v, _, _, lf, _ = _fox_proj(ys, w_qkv, w_f, b_f, seq_s, tm_s)
            o = _fox_decode(q.astype(F32), k, v, lf, pool_k, pool_v, pool_lft, page_table, j * n_pool, 4)
            parts_s = (o,)
            k_s.append(k.reshape(bs, seq_s, n_heads, HEAD)); v_s.append(v.reshape(bs, seq_s, n_heads, HEAD))
            lf_s.append(lf.reshape(bs, seq_s, n_heads))
        g1, b1, g2, b2 = row(ln1_g[layer]), row(ln1_b[layer]), row(ln2_g[layer]), row(ln2_b[layer])
        wg, wu, wd = ffn_w_gate[layer].astype(BF16), ffn_w_up[layer].astype(BF16), ffn_w_down[layer].astype(BF16)
        yp = _outproj_ln(parts_p, w_out, yp, g1, b1, alpha, tm_p)
        ys = _outproj_ln(parts_s, w_out, ys, g1, b1, alpha, tm_s)
        yp = _ffn_ln(yp, wg, wu, wd, g2, b2, alpha, tm_ffn)
        ys = _ffn_ln(ys, wg, wu, wd, g2, b2, alpha, tm_s)
    return (yp.reshape(bp, seq_p, d_model), ys.reshape(bs, seq_s, d_model),
            jnp.stack(shift_p), jnp.stack(shift_s), jnp.stack(wkv_p), jnp.stack(wkv_s),
            jnp.stack(s5re_p), jnp.stack(s5re_s), jnp.stack(s5im_p), jnp.stack(s5im_s),
            jnp.stack(k_p), jnp.stack(k_s), jnp.stack(v_p), jnp.stack(v_s),
            jnp.stack(lf_p), jnp.stack(lf_s))
```

```python
import functools
import math

import jax
import jax.numpy as jnp
from jax import lax
from jax.experimental import pallas as pl
from jax.experimental.pallas import tpu as pltpu

F32, BF16 = jnp.float32, jnp.bfloat16
HI = lax.Precision.HIGHEST

LANES = 128
SUBLANES = 8
VMEM_LIMIT_BYTES = 56 * 1024 * 1024

HEAD = 64
S5_GROUP = 16
S5_STATE = 64
LORA_WA = 128
LORA_G = 128
GN_EPS = 64e-5
LN_EPS = 1e-5
RWKV_CHUNK = 64


def _dot(a, b, precision=None):
    return jnp.dot(a, b, preferred_element_type=F32, precision=precision)


def _dot_nt(a, b, precision=None):
    return lax.dot_general(a, b, (((1,), (1,)), ((), ())), preferred_element_type=F32, precision=precision)


def _dot_tn(a, b, precision=None):
    return lax.dot_general(a, b, (((0,), (0,)), ((), ())), preferred_element_type=F32, precision=precision)


def _split(a, terms=2):
    out = []
    for _ in range(terms - 1):
        piece = a.astype(BF16)
        out.append(piece)
        a = a - piece.astype(F32)
    out.append(a.astype(BF16))
    return tuple(out)


def _mm3(a2, b2, dot=_dot):
    (a_hi, a_lo), (b_hi, b_lo) = a2, b2
    return dot(a_hi, b_hi) + dot(a_hi, b_lo) + dot(a_lo, b_hi)


def _mm_exact_rhs(a_terms, b):
    acc = None
    for piece in a_terms:
        t = _dot(piece, b)
        acc = t if acc is None else acc + t
    return acc


def _params(*semantics):
    return pltpu.CompilerParams(dimension_semantics=semantics, vmem_limit_bytes=VMEM_LIMIT_BYTES)


def _row_tile(m, preferred):
    tm = min(m, preferred)
    assert m % tm == 0, (m, tm)
    return tm


def _layer_norm(z, g, b):
    mu = jnp.mean(z, axis=-1, keepdims=True)
    zc = z - mu
    var = jnp.mean(zc * zc, axis=-1, keepdims=True)
    return zc * lax.rsqrt(var + LN_EPS) * g + b


def _softplus(z):
    return jnp.maximum(z, 0.0) + jnp.log1p(jnp.exp(-jnp.abs(z)))


def _proj_kernel(x_ref, w_ref, *o_refs, widths):
    acc = _dot(x_ref[...].astype(BF16), w_ref[...])
    off = 0
    for o_ref, n in zip(o_refs, widths):
        o_ref[...] = acc[:, off:off + n].astype(o_ref.dtype)
        off += n


def _proj(x, w, widths, dtypes, tm):
    m, k = x.shape
    n = w.shape[1]
    assert sum(widths) == n
    return pl.pallas_call(
        functools.partial(_proj_kernel, widths=tuple(widths)),
        grid=(m // tm,),
        in_specs=[pl.BlockSpec((tm, k), lambda i: (i, 0)), pl.BlockSpec((k, n), lambda i: (0, 0))],
        out_specs=[pl.BlockSpec((tm, wd), lambda i: (i, 0)) for wd in widths],
        out_shape=[jax.ShapeDtypeStruct((m, wd), dt) for wd, dt in zip(widths, dtypes)],
        compiler_params=_params("parallel"),
        name="proj_split",
    )(x, w)


def _outproj_ln_kernel(*refs, n_parts, alpha):
    part_refs = refs[:n_parts]
    w_ref, x_ref, g_ref, b_ref, o_ref = refs[n_parts:]
    off = 0
    mix = None
    for p_ref in part_refs:
        kp = p_ref.shape[1]
        t = _dot(p_ref[...].astype(BF16), w_ref[off:off + kp, :])
        mix = t if mix is None else mix + t
        off += kp
    o_ref[...] = _layer_norm(alpha * x_ref[...] + mix, g_ref[...], b_ref[...])


def _outproj_ln(parts, w, x, g, b, alpha, tm):
    m, d = x.shape
    in_specs = [pl.BlockSpec((tm, p.shape[1]), lambda i: (i, 0)) for p in parts]
    in_specs += [pl.BlockSpec(w.shape, lambda i: (0, 0)), pl.BlockSpec((tm, d), lambda i: (i, 0)),
                 pl.BlockSpec((1, d), lambda i: (0, 0)), pl.BlockSpec((1, d), lambda i: (0, 0))]
    return pl.pallas_call(
        functools.partial(_outproj_ln_kernel, n_parts=len(parts), alpha=alpha),
        grid=(m // tm,),
        in_specs=in_specs,
        out_specs=pl.BlockSpec((tm, d), lambda i: (i, 0)),
        out_shape=jax.ShapeDtypeStruct((m, d), F32),
        compiler_params=_params("parallel"),
        name="outproj_ln",
    )(*parts, w, x, g, b)


def _ffn_chunk(dff):
    for c in (512, 256, 128):
        if dff % c == 0:
            return c
    return dff


def _ffn_ln_kernel(x_ref, wg_ref, wu_ref, wd_ref, g_ref, b_ref, o_ref, *, alpha, chunk):
    x = x_ref[...]
    xb = x.astype(BF16)
    dff = wg_ref.shape[1]
    out = None
    for c in range(dff // chunk):
        cs = slice(c * chunk, (c + 1) * chunk)
        gate = _dot(xb, wg_ref[:, cs])
        up = _dot(xb, wu_ref[:, cs])
        h = (gate * jax.nn.sigmoid(gate) * up).astype(BF16)
        t = _dot(h, wd_ref[cs, :])
        out = t if out is None else out + t
    o_ref[...] = _layer_norm(alpha * x + out, g_ref[...], b_ref[...])


def _ffn_ln(x, wg, wu, wd, g, b, alpha, tm):
    m, d = x.shape
    dff = wg.shape[1]
    const = lambda i: (0, 0)
    return pl.pallas_call(
        functools.partial(_ffn_ln_kernel, alpha=alpha, chunk=_ffn_chunk(dff)),
        grid=(m // tm,),
        in_specs=[pl.BlockSpec((tm, d), lambda i: (i, 0)), pl.BlockSpec((d, dff), const),
                  pl.BlockSpec((d, dff), const), pl.BlockSpec((dff, d), const),
                  pl.BlockSpec((1, d), const), pl.BlockSpec((1, d), const)],
        out_specs=pl.BlockSpec((tm, d), lambda i: (i, 0)),
        out_shape=jax.ShapeDtypeStruct((m, d), F32),
        compiler_params=_params("parallel"),
        name="ffn_ln",
    )(x, wg, wu, wd, g, b)


def _rwkv_kernel(p_ref, sprev_ref, s0_ref, mu_ref, w0_ref, wlw_ref, a0_ref, wla_ref, wlg_ref, kk_ref, ka_ref,
                 rk_ref, gng_ref, gnb_ref, ones_ref, y_ref, shift_ref, sout_ref, carry_sc, state_sc,
                 *, chunk, t_total, n_chunks):
    c = pl.program_id(1)
    d = y_ref.shape[-1]
    n_pairs = d // LANES
    ell = chunk

    @pl.when(c == 0)
    def _():
        carry_sc[...] = sprev_ref[...]
        state_sc[...] = s0_ref[...]

    p = p_ref[...]
    row = lax.broadcasted_iota(jnp.int32, (ell, 1), 0)
    prev = jnp.where(row == 0, carry_sc[...], pltpu.roll(p, 1, 0))
    ps = p + (prev - p) * mu_ref[...]
    carry_sc[...] = p[ell - 1:ell, :]

    r = ps[:, 0:d]
    k = ps[:, d:2 * d]
    v = ps[:, 2 * d:3 * d]
    x_wa = ps[:, 3 * d:3 * d + LORA_WA]
    x_g = ps[:, 3 * d + LORA_WA:3 * d + LORA_WA + LORA_G]
    ones_blk = ones_ref[...]
    head_sum = lambda z: _mm_exact_rhs(_split(z), ones_blk)

    w_log = -_softplus(-(w0_ref[...] + _mm3(_split(jnp.tanh(x_wa)), (wlw_ref[0], wlw_ref[1])))) - 0.5
    logw = -jnp.exp(w_log)
    a = jax.nn.sigmoid(a0_ref[...] + _mm3(_split(x_wa), (wla_ref[0], wla_ref[1])))
    g = _mm3(_split(jax.nn.sigmoid(x_g)), (wlg_ref[0], wlg_ref[1]))
    kk = k * kk_ref[...]
    kk = kk / jnp.maximum(jnp.sqrt(head_sum(kk * kk)), 1e-12)
    k2 = k * (1.0 + (a - 1.0) * ka_ref[...])
    bv = kk * a
    if t_total % ell:
        valid = (c * ell + row) < t_total
        logw = jnp.where(valid, logw, 0.0)
        kk = jnp.where(valid, kk, 0.0)
        bv = jnp.where(valid, bv, 0.0)
        k2 = jnp.where(valid, k2, 0.0)

    ti = lax.broadcasted_iota(jnp.int32, (ell, ell), 0)
    tj = lax.broadcasted_iota(jnp.int32, (ell, ell), 1)
    tri = (ti >= tj).astype(BF16)
    cl = None
    for piece in _split(logw, 3):
        t = _dot(tri, piece)
        cl = t if cl is None else cl + t
    e_in = jnp.exp(cl)
    e_neg = jnp.exp(-cl)
    a_t = _split(-kk * jnp.exp(cl - logw))
    r_t = _split(r * e_in)
    b_t = _split(bv * e_neg)
    k_t = _split(k2 * e_neg)
    v_t = _split(v)
    w_end = e_in[ell - 1:ell, :]

    lane = lax.broadcasted_iota(jnp.int32, (1, LANES), 1)
    first = lane < HEAD
    si = lax.broadcasted_iota(jnp.int32, (2 * ell, 2 * ell), 0) % ell
    sj = lax.broadcasted_iota(jnp.int32, (2 * ell, 2 * ell), 1) % ell
    strict = si > sj
    incl = si >= sj
    eye = (lax.broadcasted_iota(jnp.int32, (2 * ell, 2 * ell), 0)
           == lax.broadcasted_iota(jnp.int32, (2 * ell, 2 * ell), 1)).astype(F32)

    def stack(z2, sl):
        zero = jnp.zeros((ell, LANES), BF16)
        return tuple(jnp.concatenate([jnp.where(first, z[:, sl], zero), jnp.where(first, zero, z[:, sl])], axis=0)
                     for z in z2)

    cat = lambda x2, y2: tuple(jnp.concatenate([x, y], axis=0) for x, y in zip(x2, y2))

    pairs = range(n_pairs)
    sls = [slice(pr * LANES, (pr + 1) * LANES) for pr in pairs]
    a_s, r_s, v_s = [stack(a_t, sl) for sl in sls], [stack(r_t, sl) for sl in sls], [stack(v_t, sl) for sl in sls]
    bk_s = [cat(stack(b_t, sl), stack(k_t, sl)) for sl in sls]
    state = [state_sc[pr] for pr in pairs]
    state2 = [_split(s) for s in state]
    gram = [_mm3(cat(a_s[pr], r_s[pr]), bk_s[pr], _dot_nt) for pr in pairs]
    n_ab = [jnp.where(strict, g[:2 * ell, :2 * ell], 0.0) for g in gram]
    n_ak = [jnp.where(strict, g[:2 * ell, 2 * ell:], 0.0) for g in gram]
    n_rb = [jnp.where(incl, g[2 * ell:, :2 * ell], 0.0) for g in gram]
    n_rk = [jnp.where(incl, g[2 * ell:, 2 * ell:], 0.0) for g in gram]
    inv = [eye + n for n in n_ab]
    pw = list(n_ab)
    for _ in range(int(math.log2(ell)) - 1):
        pw2 = [_split(m) for m in pw]
        pw = [_mm3(m2, m2) for m2 in pw2]
        inv = [inv[pr] + _mm3(_split(pw[pr]), _split(inv[pr])) for pr in pairs]
    x = [_mm3(a_s[pr], state2[pr], _dot_nt) + _mm3(_split(n_ak[pr]), v_s[pr]) for pr in pairs]
    u = [_mm3(_split(inv[pr]), _split(x[pr])) for pr in pairs]
    y2 = [_mm3(r_s[pr], state2[pr], _dot_nt) + _mm3(_split(n_rb[pr]), _split(u[pr])) + _mm3(_split(n_rk[pr]), v_s[pr])
          for pr in pairs]
    for pr in pairs:
        sl = sls[pr]
        v_f32 = jnp.concatenate([jnp.where(first, v[:, sl], 0.0), jnp.where(first, 0.0, v[:, sl])], axis=0)
        uv_t = jnp.concatenate([u[pr], v_f32], axis=0).T
        state_sc[pr] = (state[pr] + _mm3(_split(uv_t), bk_s[pr])) * w_end[:, sl]

    y = jnp.concatenate([m[:ell] + m[ell:] for m in y2], axis=1)
    inv_head = 1.0 / HEAD
    mean = head_sum(y) * inv_head
    yc = y - mean
    var = head_sum(yc * yc) * inv_head
    yn = yc * lax.rsqrt(var + GN_EPS) * gng_ref[...] + gnb_ref[...]
    bonus = head_sum(r * k2 * rk_ref[...]) * v
    y_ref[...] = ((yn + bonus) * g).astype(y_ref.dtype)

    @pl.when(c == n_chunks - 1)
    def _():
        last = (t_total - 1) % ell
        shift_ref[...] = p[last:last + 1, :]
        sout_ref[...] = state_sc[...]


def _pad_rows(w, rows, offset):
    out = jnp.zeros((rows, w.shape[1]), F32)
    return out.at[offset:offset + w.shape[0]].set(w.astype(F32))


def _rwkv(p, shift_prev, wkv0, lw):
    bsz, t_total, width = p.shape
    d = width - LORA_WA - LORA_G
    d //= 3
    n_pairs = d // LANES
    ell = RWKV_CHUNK
    n_chunks = -(-t_total // ell)
    t_pad = n_chunks * ell
    if t_pad != t_total:
        p = jnp.pad(p, ((0, 0), (0, t_pad - t_total), (0, 0)))
    s0 = wkv0.astype(F32).reshape(bsz, n_pairs, 2, HEAD, HEAD)
    zero = jnp.zeros_like(s0[:, :, 0])
    s0 = jnp.concatenate([jnp.concatenate([s0[:, :, 0], zero], axis=-1),
                          jnp.concatenate([zero, s0[:, :, 1]], axis=-1)], axis=-2)
    row = lambda z: z.astype(F32).reshape(1, -1)
    ones_blk = jnp.kron(jnp.eye(d // HEAD, dtype=F32), jnp.ones((HEAD, HEAD), F32)).astype(BF16)
    hi_lo = lambda w: jnp.stack(_split(w))
    consts = [row(lw["mu"]), row(lw["w0"]), hi_lo(_pad_rows(lw["w_lora_w"], LORA_WA, 0)), row(lw["a0"]),
              hi_lo(_pad_rows(lw["w_lora_a"], LORA_WA, LORA_WA // 2)), hi_lo(lw["w_lora_g"].astype(F32)),
              row(lw["k_k"]), row(lw["k_a"]), row(lw["r_k"]), row(lw["lnx_g"]), row(lw["lnx_b"]), ones_blk]
    const_specs = [pl.BlockSpec(cst.shape, lambda b, c, nd=cst.ndim: (0,) * nd) for cst in consts]
    y, shift, sout = pl.pallas_call(
        functools.partial(_rwkv_kernel, chunk=ell, t_total=t_total, n_chunks=n_chunks),
        grid=(bsz, n_chunks),
        in_specs=[pl.BlockSpec((None, ell, width), lambda b, c: (b, c, 0)),
                  pl.BlockSpec((None, 1, width), lambda b, c: (b, 0, 0)),
                  pl.BlockSpec((None, n_pairs, LANES, LANES), lambda b, c: (b, 0, 0, 0))] + const_specs,
        out_specs=[pl.BlockSpec((None, ell, d), lambda b, c: (b, c, 0)),
                   pl.BlockSpec((None, 1, width), lambda b, c: (b, 0, 0)),
                   pl.BlockSpec((None, n_pairs, LANES, LANES), lambda b, c: (b, 0, 0, 0))],
        out_shape=[jax.ShapeDtypeStruct((bsz, t_pad, d), BF16),
                   jax.ShapeDtypeStruct((bsz, 1, width), F32),
                   jax.ShapeDtypeStruct((bsz, n_pairs, LANES, LANES), F32)],
        scratch_shapes=[pltpu.VMEM((1, width), F32), pltpu.VMEM((n_pairs, LANES, LANES), F32)],
        compiler_params=_params("parallel", "arbitrary"),
        name="rwkv7_chunked",
    )(p, shift_prev.astype(F32).reshape(bsz, 1, width), s0, *consts)
    wkv = jnp.stack([sout[:, :, :HEAD, :HEAD], sout[:, :, HEAD:, HEAD:]], axis=2)
    return y[:, :t_total], shift[:, 0], wkv.reshape(bsz, 2 * n_pairs, HEAD, HEAD)


def _gelu_tanh(x):
    return 0.5 * x * (1.0 + jnp.tanh(math.sqrt(2.0 / math.pi) * (x + 0.044715 * (x * x * x))))


def _s5_kernel(u_ref, h0re_ref, h0im_ref, abre_ref, abim_ref, wbre_ref, wbim_ref, wcre_ref, wcim_ref, dskip_ref,
               wglu_ref, bglu_ref, y_ref, hre_ref, him_ref, ere_sc, eim_sc, hre_sc, him_sc,
               *, tb, n_blocks, blocks_per_group):
    tblk = pl.program_id(1)
    nb = SUBLANES
    n_state = hre_sc.shape[1]
    n_in = u_ref.shape[-1]
    n_panels = wbre_ref.shape[0]
    p_in, p_state = n_in // n_panels, n_state // n_panels

    @pl.when(tblk == 0)
    def _():
        hre_sc[...] = h0re_ref[...]
        him_sc[...] = h0im_ref[...]

    u = u_ref[...].reshape(tb * nb, n_in)
    ub = u.astype(BF16)
    for pn in range(n_panels):
        up = ub[:, pn * p_in:(pn + 1) * p_in]
        ere_sc[:, pn * p_state:(pn + 1) * p_state] = _dot(up, wbre_ref[pn])
        eim_sc[:, pn * p_state:(pn + 1) * p_state] = _dot(up, wbim_ref[pn])

    n_blk = n_state // LANES
    for cg in range(n_blk // blocks_per_group):
        blks = range(cg * blocks_per_group, (cg + 1) * blocks_per_group)
        a_re = [jnp.broadcast_to(abre_ref[:, jb * LANES:(jb + 1) * LANES], (nb, LANES)) for jb in blks]
        a_im = [jnp.broadcast_to(abim_ref[:, jb * LANES:(jb + 1) * LANES], (nb, LANES)) for jb in blks]

        def step(t, carry, blks=blks, a_re=a_re, a_im=a_im):
            rows = pl.ds(pl.multiple_of(t * nb, nb), nb)
            new_re, new_im = [], []
            for n, jb in enumerate(blks):
                cs = slice(jb * LANES, (jb + 1) * LANES)
                h_re, h_im = carry[0][n], carry[1][n]
                n_re = a_re[n] * h_re - a_im[n] * h_im + ere_sc[rows, cs]
                n_im = a_re[n] * h_im + a_im[n] * h_re + eim_sc[rows, cs]
                ere_sc[rows, cs] = n_re
                eim_sc[rows, cs] = n_im
                new_re.append(n_re)
                new_im.append(n_im)
            return tuple(new_re), tuple(new_im)

        init = (tuple(hre_sc[:, jb * LANES:(jb + 1) * LANES] for jb in blks),
                tuple(him_sc[:, jb * LANES:(jb + 1) * LANES] for jb in blks))
        fin_re, fin_im = lax.fori_loop(0, tb, step, init, unroll=2 if tb % 2 == 0 else 1)
        for n, jb in enumerate(blks):
            hre_sc[:, jb * LANES:(jb + 1) * LANES] = fin_re[n]
            him_sc[:, jb * LANES:(jb + 1) * LANES] = fin_im[n]

    y_parts = []
    for pn in range(n_panels):
        ss = slice(pn * p_state, (pn + 1) * p_state)
        y_parts.append(_dot(ere_sc[:, ss].astype(BF16), wcre_ref[pn]) - _dot(eim_sc[:, ss].astype(BF16), wcim_ref[pn]))
    y = jnp.concatenate(y_parts, axis=1) + dskip_ref[...] * u
    y = _gelu_tanh(y)
    y = y * jax.nn.sigmoid(_dot(y.astype(BF16), wglu_ref[...]) + bglu_ref[...])
    y_ref[...] = y.reshape(tb, nb, n_in).astype(y_ref.dtype)

    @pl.when(tblk == n_blocks - 1)
    def _():
        hre_ref[...] = hre_sc[...]
        him_ref[...] = him_sc[...]


def _s5_weights(lw):
    lr, li = lw["s5_lambda_re"].astype(F32), lw["s5_lambda_im"].astype(F32)
    n_groups = lr.shape[0]
    dt = jnp.exp(lw["s5_log_dt"].astype(F32))[:, None]
    mag = jnp.exp(lr * dt)
    ab_re, ab_im = mag * jnp.cos(li * dt), mag * jnp.sin(li * dt)
    den = lr * lr + li * li
    n_re = ab_re - 1.0
    f_re = (n_re * lr + ab_im * li) / den
    f_im = (ab_im * lr - n_re * li) / den
    br, bi = lw["s5_b_re"].astype(F32), lw["s5_b_im"].astype(F32)
    bb_re = f_re[..., None] * br - f_im[..., None] * bi
    bb_im = f_re[..., None] * bi + f_im[..., None] * br
    eye = jnp.eye(n_groups, dtype=F32)
    n_in, n_state = n_groups * S5_GROUP, n_groups * S5_STATE
    n_panels = max(1, n_in // (2 * LANES))
    p_in, p_state = n_in // n_panels, n_state // n_panels

    def panels(w, rows, cols):
        return jnp.stack([w[pn * rows:(pn + 1) * rows, pn * cols:(pn + 1) * cols] for pn in range(n_panels)])

    expand_b = lambda z: panels(jnp.einsum("gpc,gh->gchp", z, eye).reshape(n_in, n_state), p_in, p_state).astype(BF16)
    expand_c = lambda z: panels(jnp.einsum("gcp,gh->gphc", z.astype(F32), eye).reshape(n_state, n_in),
                                p_state, p_in).astype(BF16)
    return dict(ab_re=ab_re.reshape(1, n_state), ab_im=ab_im.reshape(1, n_state),
                wb_re=expand_b(bb_re), wb_im=expand_b(bb_im),
                wc_re=expand_c(lw["s5_c_re"]), wc_im=expand_c(lw["s5_c_im"]),
                d_skip=lw["s5_d"].astype(F32).reshape(1, n_in),
                w_glu=lw["s5_w_glu"].astype(BF16), b_glu=lw["s5_b_glu"].astype(F32).reshape(1, n_in))


def _s5(u, h_re0, h_im0, sw, tb):
    bsz, t_total, n_in = u.shape
    n_state = sw["ab_re"].shape[1]
    nb = SUBLANES
    assert bsz % nb == 0 and t_total % tb == 0
    n_blocks = t_total // tb
    blocks_per_group = math.gcd(n_state // LANES, 8)
    consts = [sw["ab_re"], sw["ab_im"], sw["wb_re"], sw["wb_im"], sw["wc_re"], sw["wc_im"], sw["d_skip"],
              sw["w_glu"], sw["b_glu"]]
    seq_block = pl.BlockSpec((tb, nb, n_in), lambda g, t: (t, g, 0))
    state_block = pl.BlockSpec((nb, n_state), lambda g, t: (g, 0))
    y, h_re, h_im = pl.pallas_call(
        functools.partial(_s5_kernel, tb=tb, n_blocks=n_blocks, blocks_per_group=blocks_per_group),
        grid=(bsz // nb, n_blocks),
        in_specs=[seq_block, state_block, state_block]
                 + [pl.BlockSpec(cst.shape, lambda g, t, nd=cst.ndim: (0,) * nd) for cst in consts],
        out_specs=[seq_block, state_block, state_block],
        out_shape=[jax.ShapeDtypeStruct((t_total, bsz, n_in), BF16),
                   jax.ShapeDtypeStruct((bsz, n_state), F32),
                   jax.ShapeDtypeStruct((bsz, n_state), F32)],
        scratch_shapes=[pltpu.VMEM((nb * tb, n_state), F32), pltpu.VMEM((nb * tb, n_state), F32),
                        pltpu.VMEM((nb, n_state), F32), pltpu.VMEM((nb, n_state), F32)],
        compiler_params=_params("parallel", "arbitrary"),
        name="s5_scan",
    )(jnp.swapaxes(u, 0, 1), h_re0.astype(F32).reshape(bsz, n_state), h_im0.astype(F32).reshape(bsz, n_state),
      *consts)
    shape = (bsz, n_state // S5_STATE, S5_STATE)
    return jnp.swapaxes(y, 0, 1), h_re.reshape(shape), h_im.reshape(shape)


def _log_sigmoid(z):
    return jnp.minimum(z, 0.0) - jnp.log1p(jnp.exp(-jnp.abs(z)))


def _fox_proj_kernel(x_ref, w_ref, wf_ref, bf_ref, q_ref, k_ref, v_ref, kb_ref, vb_ref, lf_ref, c_ref, carry_sc,
                     *, d, scale, tiles_per_seq):
    i = pl.program_id(0)
    tm = x_ref.shape[0]
    xb = x_ref[...].astype(BF16)
    acc = _dot(xb, w_ref[...])
    q_ref[...] = (acc[:, :d] * scale).astype(BF16)
    k = acc[:, d:2 * d]
    v = acc[:, 2 * d:3 * d]
    k_ref[...] = k
    v_ref[...] = v
    kb_ref[...] = k.astype(BF16)
    vb_ref[...] = v.astype(BF16)
    lf = _log_sigmoid(_dot(xb, wf_ref[...]) + bf_ref[...])
    lf_ref[...] = lf

    @pl.when(i % tiles_per_seq == 0)
    def _():
        carry_sc[...] = jnp.zeros_like(carry_sc)

    ti = lax.broadcasted_iota(jnp.int32, (tm, tm), 0)
    tj = lax.broadcasted_iota(jnp.int32, (tm, tm), 1)
    c = _dot((ti >= tj).astype(F32), lf, HI) + carry_sc[...]
    c_ref[...] = c
    carry_sc[...] = c[tm - 1:tm, :]


def _fox_proj(x, w_qkv, w_f, b_f, seq_len, tm):
    m, kdim = x.shape
    d = w_qkv.shape[1] // 3
    nh = w_f.shape[1]
    assert seq_len % tm == 0 or tm % seq_len == 0
    row = lambda n: pl.BlockSpec((tm, n), lambda i: (i, 0))
    return pl.pallas_call(
        functools.partial(_fox_proj_kernel, d=d, scale=HEAD ** -0.5, tiles_per_seq=max(seq_len // tm, 1)),
        grid=(m // tm,),
        in_specs=[row(kdim), pl.BlockSpec(w_qkv.shape, lambda i: (0, 0)), pl.BlockSpec(w_f.shape, lambda i: (0, 0)),
                  pl.BlockSpec((1, nh), lambda i: (0, 0))],
        out_specs=[row(d), row(d), row(d), row(d), row(d), row(nh), row(nh)],
        out_shape=[jax.ShapeDtypeStruct((m, d), BF16), jax.ShapeDtypeStruct((m, d), F32),
                   jax.ShapeDtypeStruct((m, d), F32), jax.ShapeDtypeStruct((m, d), BF16),
                   jax.ShapeDtypeStruct((m, d), BF16), jax.ShapeDtypeStruct((m, nh), F32),
                   jax.ShapeDtypeStruct((m, nh), F32)],
        scratch_shapes=[pltpu.VMEM((1, nh), F32)],
        compiler_params=_params("arbitrary"),
        name="fox_proj",
    )(x, w_qkv, w_f, b_f)


def _fox_attn_kernel(q_ref, k_ref, v_ref, c_ref, o_ref, m_sc, l_sc, acc_sc, *, tq, rq):
    qi = pl.program_id(2)
    n_rb = tq // rq
    heads = LANES // HEAD
    lane = lax.broadcasted_iota(jnp.int32, (1, LANES), 1)
    q = q_ref[...]
    qm = [jnp.where((lane // HEAD) == h, q, jnp.zeros_like(q)) for h in range(heads)]
    m_sc[...] = jnp.full_like(m_sc, -jnp.inf)
    l_sc[...] = jnp.zeros_like(l_sc)
    acc_sc[...] = jnp.zeros_like(acc_sc)

    def chain(h, rb, start, width, masked):
        rows = slice(rb * rq, (rb + 1) * rq)
        kt = k_ref[pl.ds(start, width), :]
        vt = v_ref[pl.ds(start, width), :]
        s = _dot_nt(qm[h][rows], kt) - c_ref[h, :, pl.ds(start, width)]
        if masked:
            ri = lax.broadcasted_iota(jnp.int32, (rq, width), 0) + rb * rq
            ci = lax.broadcasted_iota(jnp.int32, (rq, width), 1)
            s = jnp.where(ci <= ri, s, -jnp.inf)
        m_prev = m_sc[h, rows, :]
        m_new = jnp.maximum(m_prev, jnp.max(s, axis=1, keepdims=True))
        alpha = jnp.exp(m_prev - m_new)
        p = jnp.exp(s - jnp.concatenate([m_new] * (width // LANES), axis=1))
        l_sc[h, rows, :] = alpha * l_sc[h, rows, :] + jnp.sum(p, axis=1, keepdims=True)
        acc_sc[h, rows, :] = alpha * acc_sc[h, rows, :] + _dot(p.astype(BF16), vt)
        m_sc[h, rows, :] = m_new

    def body(ki, carry):
        start = pl.multiple_of(ki * tq, tq)
        for h in range(heads):
            for rb in range(n_rb):
                chain(h, rb, start, tq, False)
        return carry

    lax.fori_loop(0, qi, body, 0)
    start = pl.multiple_of(qi * tq, tq)
    for h in range(heads):
        for rb in range(n_rb):
            chain(h, rb, start, (rb + 1) * rq, True)
    out = [acc_sc[h] / l_sc[h] for h in range(heads)]
    o_ref[...] = jnp.where(lane < HEAD, out[0], out[1]).astype(o_ref.dtype)


def _fox_attn(q, kb, vb, c, tq):
    bsz, t_total, d = q.shape
    n_pairs = d // LANES
    hp = LANES // HEAD
    return pl.pallas_call(
        functools.partial(_fox_attn_kernel, tq=tq, rq=tq),
        grid=(bsz, n_pairs, t_total // tq),
        in_specs=[pl.BlockSpec((None, tq, LANES), lambda b, h, i: (b, i, h)),
                  pl.BlockSpec((None, t_total, LANES), lambda b, h, i: (b, 0, h)),
                  pl.BlockSpec((None, t_total, LANES), lambda b, h, i: (b, 0, h)),
                  pl.BlockSpec((None, hp, 1, t_total), lambda b, h, i: (b, h, 0, 0))],
        out_specs=pl.BlockSpec((None, tq, LANES), lambda b, h, i: (b, i, h)),
        out_shape=jax.ShapeDtypeStruct((bsz, t_total, d), BF16),
        scratch_shapes=[pltpu.VMEM((hp, tq, LANES), F32)] * 3,
        compiler_params=_params("parallel", "parallel", "arbitrary"),
        name="fox_attn",
    )(q, kb, vb, c)


def _fox_decode_kernel(pt_ref, qh_ref, knew_ref, vnew_ref, lfnew_ref, *refs, pages_per_step, n_steps):
    del pt_ref
    npp = pages_per_step
    k_refs, v_refs, lf_refs = refs[:npp], refs[npp:2 * npp], refs[2 * npp:3 * npp]
    o_ref, m_sc, l_sc, acc_sc, suf_sc = refs[3 * npp:]
    j = pl.program_id(1)

    @pl.when(j == 0)
    def _():
        m_sc[...] = jnp.full_like(m_sc, -jnp.inf)
        l_sc[...] = jnp.zeros_like(l_sc)
        acc_sc[...] = jnp.zeros_like(acc_sc)
        suf_sc[...] = jnp.zeros_like(suf_sc)

    qh = qh_ref[...]
    qb = qh.astype(BF16)
    c_new = lfnew_ref[...]
    page = k_refs[0].shape[0]
    later = (lax.broadcasted_iota(jnp.int32, (page, page), 0)
             > lax.broadcasted_iota(jnp.int32, (page, page), 1)).astype(F32)
    for i in range(npp):
        kp = k_refs[i][...].astype(BF16)
        vp = v_refs[i][...].astype(BF16)
        lft = lf_refs[i][...]
        s = _dot_nt(qb, kp) + c_new + _dot(lft, later, HI) + suf_sc[...]
        m_prev = m_sc[...]
        m_new = jnp.maximum(m_prev, jnp.max(s, axis=1, keepdims=True))
        alpha = jnp.exp(m_prev - m_new)
        p = jnp.exp(s - m_new)
        l_sc[...] = alpha * l_sc[...] + jnp.sum(p, axis=1, keepdims=True)
        acc_sc[...] = alpha * acc_sc[...] + _dot(p.astype(BF16), vp)
        m_sc[...] = m_new
        suf_sc[...] = suf_sc[...] + jnp.sum(lft, axis=1, keepdims=True)

    @pl.when(j == n_steps - 1)
    def _():
        s_new = jnp.sum(qh * knew_ref[...], axis=1, keepdims=True)
        m_prev = m_sc[...]
        m_new = jnp.maximum(m_prev, s_new)
        alpha = jnp.exp(m_prev - m_new)
        p_new = jnp.exp(s_new - m_new)
        l_fin = alpha * l_sc[...] + p_new
        o_heads = (alpha * acc_sc[...] + p_new * vnew_ref[...]) / l_fin
        nh, d = o_heads.shape
        own = (lax.broadcasted_iota(jnp.int32, (nh, d), 1) // HEAD) == lax.broadcasted_iota(jnp.int32, (nh, d), 0)
        o_ref[...] = jnp.sum(jnp.where(own, o_heads, 0.0), axis=0, keepdims=True)


def _fox_decode(q, k_new, v_new, lf_new, pool_k, pool_v, pool_lft, page_table, layer_offset, pages_per_step):
    bsz, d = q.shape
    nh = d // HEAD
    n_pages = page_table.shape[1]
    page = pool_k.shape[1]
    npp = pages_per_step
    assert n_pages % npp == 0
    n_steps = n_pages // npp
    own = (jnp.arange(d)[None, :] // HEAD) == jnp.arange(nh)[:, None]
    qh = jnp.where(own[None], q[:, None, :], 0.0)
    pt = page_table.reshape(-1).astype(jnp.int32)

    def page_map(i):
        return lambda b, j, pt_ref: (pt_ref[b * n_pages + (n_pages - 1 - (j * npp + i))] + layer_offset, 0, 0)

    per_seq = lambda n0, n1: pl.BlockSpec((None, n0, n1), lambda b, j, pt_ref: (b, 0, 0))
    in_specs = [per_seq(nh, d), per_seq(1, d), per_seq(1, d), per_seq(nh, 1)]
    in_specs += [pl.BlockSpec((None, page, d), page_map(i)) for i in range(npp)]
    in_specs += [pl.BlockSpec((None, page, d), page_map(i)) for i in range(npp)]
    in_specs += [pl.BlockSpec((None, nh, page), page_map(i)) for i in range(npp)]
    out = pl.pallas_call(
        functools.partial(_fox_decode_kernel, pages_per_step=npp, n_steps=n_steps),
        grid_spec=pltpu.PrefetchScalarGridSpec(
            num_scalar_prefetch=1,
            grid=(bsz, n_steps),
            in_specs=in_specs,
            out_specs=per_seq(1, d),
            scratch_shapes=[pltpu.VMEM((nh, 1), F32), pltpu.VMEM((nh, 1), F32), pltpu.VMEM((nh, d), F32),
                            pltpu.VMEM((nh, 1), F32)]),
        out_shape=jax.ShapeDtypeStruct((bsz, 1, d), F32),
        compiler_params=_params("parallel", "arbitrary"),
        name="fox_decode",
    )(pt, qh, k_new.reshape(bsz, 1, d), v_new.reshape(bsz, 1, d), lf_new.reshape(bsz, nh, 1),
      *([pool_k] * npp), *([pool_v] * npp), *([pool_lft] * npp))
    return out[:, 0]


def _mixer_ab(x2d, bsz, seq, shift_prev, wkv0, h_re0, h_im0, lw, tm, s5_tb):
    d_rwkv = lw["mu"].shape[0]
    d_s5 = lw["w_in"].shape[1] - d_rwkv
    p, u = _proj(x2d, lw["w_in"], (d_rwkv, d_s5), (F32, F32), tm)
    y_a, shift, wkv = _rwkv(p.reshape(bsz, seq, d_rwkv), shift_prev, wkv0, lw)
    y_b, h_re, h_im = _s5(u.reshape(bsz, seq, d_s5), h_re0, h_im0, lw["s5"], s5_tb)
    return (y_a.reshape(bsz * seq, -1), y_b.reshape(bsz * seq, -1)), shift, wkv, h_re, h_im


def kernel(x_prompt, x_sample, state_shift, state_wkv, state_s5_re, state_s5_im, cache_k, cache_v, cache_logf, page_table, w_in_ab, mu_shift, w0, w_lora_w, a0, w_lora_a, w_lora_g, k_k, k_a, r_k, lnx_g, lnx_b, s5_lambda_re, s5_lambda_im, s5_log_dt, s5_b_re, s5_b_im, s5_c_re, s5_c_im, s5_d, s5_w_glu, s5_b_glu, w_out_ab, w_in_c, b_forget, w_out_c, ln1_g, ln1_b, ln2_g, ln2_b, ffn_w_gate, ffn_w_up, ffn_w_down):
    bp, seq_p, d_model = x_prompt.shape
    bs, seq_s, _ = x_sample.shape
    assert seq_s == 1
    depth = ln1_g.shape[0]
    alpha = (2.0 * depth) ** 0.25
    n_heads = d_model // HEAD
    mp, ms = bp * seq_p, bs * seq_s
    tm_p, tm_s = _row_tile(mp, 512), _row_tile(ms, 512)
    tm_ffn = _row_tile(mp, 256)
    row = lambda z: z.astype(F32).reshape(1, -1)

    yp = x_prompt.reshape(mp, d_model)
    ys = x_sample.reshape(ms, d_model)
    n_pool, page = cache_k.shape[1], cache_k.shape[2]
    pool_k = cache_k.reshape(-1, page, d_model)
    pool_v = cache_v.reshape(-1, page, d_model)
    pool_lft = jnp.swapaxes(cache_logf.reshape(-1, page, n_heads), 1, 2)

    shift_p, shift_s, wkv_p, wkv_s = [], [], [], []
    s5re_p, s5re_s, s5im_p, s5im_s = [], [], [], []
    k_p, k_s, v_p, v_s, lf_p, lf_s = [], [], [], [], [], []
    for layer in range(depth):
        j = layer // 2
        if layer % 2 == 0:
            lw = dict(w_in=w_in_ab[j].astype(BF16), mu=mu_shift[j], w0=w0[j], w_lora_w=w_lora_w[j], a0=a0[j],
                      w_lora_a=w_lora_a[j], w_lora_g=w_lora_g[j], k_k=k_k[j], k_a=k_a[j], r_k=r_k[j],
                      lnx_g=lnx_g[j], lnx_b=lnx_b[j], s5_lambda_re=s5_lambda_re[j], s5_lambda_im=s5_lambda_im[j],
                      s5_log_dt=s5_log_dt[j], s5_b_re=s5_b_re[j], s5_b_im=s5_b_im[j], s5_c_re=s5_c_re[j],
                      s5_c_im=s5_c_im[j], s5_d=s5_d[j], s5_w_glu=s5_w_glu[j], s5_b_glu=s5_b_glu[j])
            lw["s5"] = _s5_weights(lw)
            w_out = w_out_ab[j].astype(BF16)
            n_s5 = lw["s5"]["ab_re"].shape[1]
            zeros = lambda *shape: jnp.zeros(shape, F32)
            parts_p, sh, wk, hr, hi = _mixer_ab(
                yp, bp, seq_p, zeros(bp, mu_shift.shape[1]), zeros(bp, (w0.shape[1]) // HEAD, HEAD, HEAD),
                zeros(bp, n_s5), zeros(bp, n_s5), lw, tm_p, min(seq_p, 128))
            shift_p.append(sh); wkv_p.append(wk); s5re_p.append(hr); s5im_p.append(hi)
            parts_s, sh, wk, hr, hi = _mixer_ab(
                ys, bs, seq_s, state_shift[j], state_wkv[j], state_s5_re[j], state_s5_im[j], lw, tm_s, 1)
            shift_s.append(sh); wkv_s.append(wk); s5re_s.append(hr); s5im_s.append(hi)
        else:
            w_c = w_in_c[j].astype(BF16)
            w_qkv, w_f = w_c[:, :3 * d_model], w_c[:, 3 * d_model:]
            b_f = row(b_forget[j])
            w_out = w_out_c[j].astype(BF16)
            q, k, v, kb, vb, lf, c = _fox_proj(yp, w_qkv, w_f, b_f, seq_p, tm_p)
            c = jnp.swapaxes(c.reshape(bp, seq_p, n_heads), 1, 2).reshape(bp, n_heads, 1, seq_p)
            o = _fox_attn(q.reshape(bp, seq_p, d_model), kb.reshape(bp, seq_p, d_model),
                          vb.reshape(bp, seq_p, d_model), c, min(seq_p, 512))
            parts_p = (o.reshape(mp, d_model),)
            k_p.append(k.reshape(bp, seq_p, n_heads, HEAD)); v_p.append(v.reshape(bp, seq_p, n_heads, HEAD))
            lf_p.append(lf.reshape(bp, seq_p, n_heads))
            q, k, v, _, _, lf, _ = _fox_proj(ys, w_qkv, w_f, b_f, seq_s, tm_s)
            o = _fox_decode(q.astype(F32), k, v, lf, pool_k, pool_v, pool_lft, page_table, j * n_pool, 4)
            parts_s = (o,)
            k_s.append(k.reshape(bs, seq_s, n_heads, HEAD)); v_s.append(v.reshape(bs, seq_s, n_heads, HEAD))
            lf_s.append(lf.reshape(bs, seq_s, n_heads))
        g1, b1, g2, b2 = row(ln1_g[layer]), row(ln1_b[layer]), row(ln2_g[layer]), row(ln2_b[layer])
        wg, wu, wd = ffn_w_gate[layer].astype(BF16), ffn_w_up[layer].astype(BF16), ffn_w_down[layer].astype(BF16)
        yp = _outproj_ln(parts_p, w_out, yp, g1, b1, alpha, tm_p)
        ys = _outproj_ln(parts_s, w_out, ys, g1, b1, alpha, tm_s)
        yp = _ffn_ln(yp, wg, wu, wd, g2, b2, alpha, tm_ffn)
        ys = _ffn_ln(ys, wg, wu, wd, g2, b2, alpha, tm_s)
    return (yp.reshape(bp, seq_p, d_model), ys.reshape(bs, seq_s, d_model),
            jnp.stack(shift_p), jnp.stack(shift_s), jnp.stack(wkv_p), jnp.stack(wkv_s),
            jnp.stack(s5re_p), jnp.stack(s5re_s), jnp.stack(s5im_p), jnp.stack(s5im_s),
            jnp.stack(k_p), jnp.stack(k_s), jnp.stack(v_p), jnp.stack(v_s),
            jnp.stack(lf_p), jnp.stack(lf_s))
```

```python
import functools
import math

import jax
import jax.numpy as jnp
from jax import lax
from jax.experimental import pallas as pl
from jax.experimental.pallas import tpu as pltpu

F32, BF16 = jnp.float32, jnp.bfloat16
HI = lax.Precision.HIGHEST

LANES = 128
SUBLANES = 8
VMEM_LIMIT_BYTES = 56 * 1024 * 1024

HEAD = 64
S5_GROUP = 16
S5_STATE = 64
LORA_WA = 128
LORA_G = 128
GN_EPS = 64e-5
LN_EPS = 1e-5
RWKV_CHUNK = 64
LOG2E = math.log2(math.e)


def _dot(a, b, precision=None):
    return jnp.dot(a, b, preferred_element_type=F32, precision=precision)


def _dot_nt(a, b, precision=None):
    return lax.dot_general(a, b, (((1,), (1,)), ((), ())), preferred_element_type=F32, precision=precision)


def _dot_tn(a, b, precision=None):
    return lax.dot_general(a, b, (((0,), (0,)), ((), ())), preferred_element_type=F32, precision=precision)


def _split(a, terms=2):
    out = []
    for _ in range(terms - 1):
        piece = a.astype(BF16)
        out.append(piece)
        a = a - piece.astype(F32)
    out.append(a.astype(BF16))
    return tuple(out)


def _mm3(a2, b2, dot=_dot):
    (a_hi, a_lo), (b_hi, b_lo) = a2, b2
    return dot(a_hi, b_hi) + dot(a_hi, b_lo) + dot(a_lo, b_hi)


def _mm_exact_rhs(a_terms, b):
    acc = None
    for piece in a_terms:
        t = _dot(piece, b)
        acc = t if acc is None else acc + t
    return acc


def _params(*semantics):
    return pltpu.CompilerParams(dimension_semantics=semantics, vmem_limit_bytes=VMEM_LIMIT_BYTES)


def _row_tile(m, preferred):
    tm = min(m, preferred)
    assert m % tm == 0, (m, tm)
    return tm


def _layer_norm(z, g, b):
    mu = jnp.mean(z, axis=-1, keepdims=True)
    zc = z - mu
    var = jnp.mean(zc * zc, axis=-1, keepdims=True)
    return zc * lax.rsqrt(var + LN_EPS) * g + b


def _softplus(z):
    return jnp.maximum(z, 0.0) + jnp.log1p(jnp.exp(-jnp.abs(z)))


def _proj_kernel(x_ref, w_ref, *o_refs, widths):
    acc = _dot(x_ref[...].astype(BF16), w_ref[...])
    off = 0
    for o_ref, n in zip(o_refs, widths):
        o_ref[...] = acc[:, off:off + n].astype(o_ref.dtype)
        off += n


def _proj(x, w, widths, dtypes, tm):
    m, k = x.shape
    n = w.shape[1]
    assert sum(widths) == n
    return pl.pallas_call(
        functools.partial(_proj_kernel, widths=tuple(widths)),
        grid=(m // tm,),
        in_specs=[pl.BlockSpec((tm, k), lambda i: (i, 0)), pl.BlockSpec((k, n), lambda i: (0, 0))],
        out_specs=[pl.BlockSpec((tm, wd), lambda i: (i, 0)) for wd in widths],
        out_shape=[jax.ShapeDtypeStruct((m, wd), dt) for wd, dt in zip(widths, dtypes)],
        compiler_params=_params("parallel"),
        name="proj_split",
    )(x, w)


def _outproj_ln_kernel(*refs, n_parts, alpha):
    part_refs = refs[:n_parts]
    w_ref, x_ref, g_ref, b_ref, o_ref = refs[n_parts:]
    off = 0
    mix = None
    for p_ref in part_refs:
        kp = p_ref.shape[1]
        t = _dot(p_ref[...].astype(BF16), w_ref[off:off + kp, :])
        mix = t if mix is None else mix + t
        off += kp
    o_ref[...] = _layer_norm(alpha * x_ref[...] + mix, g_ref[...], b_ref[...])


def _outproj_ln(parts, w, x, g, b, alpha, tm):
    m, d = x.shape
    in_specs = [pl.BlockSpec((tm, p.shape[1]), lambda i: (i, 0)) for p in parts]
    in_specs += [pl.BlockSpec(w.shape, lambda i: (0, 0)), pl.BlockSpec((tm, d), lambda i: (i, 0)),
                 pl.BlockSpec((1, d), lambda i: (0, 0)), pl.BlockSpec((1, d), lambda i: (0, 0))]
    return pl.pallas_call(
        functools.partial(_outproj_ln_kernel, n_parts=len(parts), alpha=alpha),
        grid=(m // tm,),
        in_specs=in_specs,
        out_specs=pl.BlockSpec((tm, d), lambda i: (i, 0)),
        out_shape=jax.ShapeDtypeStruct((m, d), F32),
        compiler_params=_params("parallel"),
        name="outproj_ln",
    )(*parts, w, x, g, b)


def _ffn_chunk(dff):
    for c in (512, 256, 128):
        if dff % c == 0:
            return c
    return dff


def _ffn_ln_kernel(x_ref, wg_ref, wu_ref, wd_ref, g_ref, b_ref, o_ref, *, alpha, chunk):
    x = x_ref[...]
    xb = x.astype(BF16)
    dff = wg_ref.shape[1]
    out = None
    for c in range(dff // chunk):
        cs = slice(c * chunk, (c + 1) * chunk)
        gate = _dot(xb, wg_ref[:, cs])
        up = _dot(xb, wu_ref[:, cs])
        h = (gate * jax.nn.sigmoid(gate) * up).astype(BF16)
        t = _dot(h, wd_ref[cs, :])
        out = t if out is None else out + t
    o_ref[...] = _layer_norm(alpha * x + out, g_ref[...], b_ref[...])


def _ffn_ln(x, wg, wu, wd, g, b, alpha, tm):
    m, d = x.shape
    dff = wg.shape[1]
    const = lambda i: (0, 0)
    return pl.pallas_call(
        functools.partial(_ffn_ln_kernel, alpha=alpha, chunk=_ffn_chunk(dff)),
        grid=(m // tm,),
        in_specs=[pl.BlockSpec((tm, d), lambda i: (i, 0)), pl.BlockSpec((d, dff), const),
                  pl.BlockSpec((d, dff), const), pl.BlockSpec((dff, d), const),
                  pl.BlockSpec((1, d), const), pl.BlockSpec((1, d), const)],
        out_specs=pl.BlockSpec((tm, d), lambda i: (i, 0)),
        out_shape=jax.ShapeDtypeStruct((m, d), F32),
        compiler_params=_params("parallel"),
        name="ffn_ln",
    )(x, wg, wu, wd, g, b)


def _rwkv_kernel(p_ref, sprev_ref, s0_ref, mu_ref, w0_ref, wlw_ref, a0_ref, wla_ref, wlg_ref, kk_ref, ka_ref,
                 rk_ref, gng_ref, gnb_ref, ones_ref, y_ref, shift_ref, sout_ref, carry_sc, state_sc,
                 *, chunk, t_total, n_chunks):
    c = pl.program_id(1)
    d = y_ref.shape[-1]
    n_pairs = d // LANES
    ell = chunk

    @pl.when(c == 0)
    def _():
        carry_sc[...] = sprev_ref[...]
        state_sc[...] = s0_ref[...]

    p = p_ref[...]
    row = lax.broadcasted_iota(jnp.int32, (ell, 1), 0)
    prev = jnp.where(row == 0, carry_sc[...], pltpu.roll(p, 1, 0))
    ps = p + (prev - p) * mu_ref[...]
    carry_sc[...] = p[ell - 1:ell, :]

    r = ps[:, 0:d]
    k = ps[:, d:2 * d]
    v = ps[:, 2 * d:3 * d]
    x_wa = ps[:, 3 * d:3 * d + LORA_WA]
    x_g = ps[:, 3 * d + LORA_WA:3 * d + LORA_WA + LORA_G]
    ones_blk = ones_ref[...]
    head_sum = lambda z: _mm_exact_rhs(_split(z), ones_blk)

    w_log = -_softplus(-(w0_ref[...] + _mm3(_split(jnp.tanh(x_wa)), (wlw_ref[0], wlw_ref[1])))) - 0.5
    logw = -jnp.exp(w_log)
    a = jax.nn.sigmoid(a0_ref[...] + _mm3(_split(x_wa), (wla_ref[0], wla_ref[1])))
    g = _mm3(_split(jax.nn.sigmoid(x_g)), (wlg_ref[0], wlg_ref[1]))
    kk = k * kk_ref[...]
    kk = kk / jnp.maximum(jnp.sqrt(head_sum(kk * kk)), 1e-12)
    k2 = k * (1.0 + (a - 1.0) * ka_ref[...])
    bv = kk * a
    if t_total % ell:
        valid = (c * ell + row) < t_total
        logw = jnp.where(valid, logw, 0.0)
        kk = jnp.where(valid, kk, 0.0)
        bv = jnp.where(valid, bv, 0.0)
        k2 = jnp.where(valid, k2, 0.0)

    ti = lax.broadcasted_iota(jnp.int32, (ell, ell), 0)
    tj = lax.broadcasted_iota(jnp.int32, (ell, ell), 1)
    tri = (ti >= tj).astype(BF16)
    cl = None
    for piece in _split(logw, 3):
        t = _dot(tri, piece)
        cl = t if cl is None else cl + t
    e_in = jnp.exp(cl)
    e_neg = jnp.exp(-cl)
    a_t = _split(-kk * jnp.exp(cl - logw))
    r_t = _split(r * e_in)
    b_t = _split(bv * e_neg)
    k_t = _split(k2 * e_neg)
    v_t = _split(v)
    w_end = e_in[ell - 1:ell, :]

    lane = lax.broadcasted_iota(jnp.int32, (1, LANES), 1)
    first = lane < HEAD
    si = lax.broadcasted_iota(jnp.int32, (2 * ell, 2 * ell), 0) % ell
    sj = lax.broadcasted_iota(jnp.int32, (2 * ell, 2 * ell), 1) % ell
    strict = si > sj
    incl = si >= sj
    eye = (lax.broadcasted_iota(jnp.int32, (2 * ell, 2 * ell), 0)
           == lax.broadcasted_iota(jnp.int32, (2 * ell, 2 * ell), 1)).astype(F32)

    def stack(z2, sl):
        zero = jnp.zeros((ell, LANES), BF16)
        return tuple(jnp.concatenate([jnp.where(first, z[:, sl], zero), jnp.where(first, zero, z[:, sl])], axis=0)
                     for z in z2)

    cat = lambda x2, y2: tuple(jnp.concatenate([x, y], axis=0) for x, y in zip(x2, y2))

    pairs = range(n_pairs)
    sls = [slice(pr * LANES, (pr + 1) * LANES) for pr in pairs]
    a_s, r_s, v_s = [stack(a_t, sl) for sl in sls], [stack(r_t, sl) for sl in sls], [stack(v_t, sl) for sl in sls]
    bk_s = [cat(stack(b_t, sl), stack(k_t, sl)) for sl in sls]
    state = [state_sc[pr] for pr in pairs]
    state2 = [_split(s) for s in state]
    gram = [_mm3(cat(a_s[pr], r_s[pr]), bk_s[pr], _dot_nt) for pr in pairs]
    n_ab = [jnp.where(strict, g[:2 * ell, :2 * ell], 0.0) for g in gram]
    n_ak = [jnp.where(strict, g[:2 * ell, 2 * ell:], 0.0) for g in gram]
    n_rb = [jnp.where(incl, g[2 * ell:, :2 * ell], 0.0) for g in gram]
    n_rk = [jnp.where(incl, g[2 * ell:, 2 * ell:], 0.0) for g in gram]
    inv = [eye + n for n in n_ab]
    pw = list(n_ab)
    for _ in range(int(math.log2(ell)) - 1):
        pw2 = [_split(m) for m in pw]
        pw = [_mm3(m2, m2) for m2 in pw2]
        inv = [inv[pr] + _mm3(_split(pw[pr]), _split(inv[pr])) for pr in pairs]
    x = [_mm3(a_s[pr], state2[pr], _dot_nt) + _mm3(_split(n_ak[pr]), v_s[pr]) for pr in pairs]
    u = [_mm3(_split(inv[pr]), _split(x[pr])) for pr in pairs]
    y2 = [_mm3(r_s[pr], state2[pr], _dot_nt) + _mm3(_split(n_rb[pr]), _split(u[pr])) + _mm3(_split(n_rk[pr]), v_s[pr])
          for pr in pairs]
    for pr in pairs:
        sl = sls[pr]
        v_f32 = jnp.concatenate([jnp.where(first, v[:, sl], 0.0), jnp.where(first, 0.0, v[:, sl])], axis=0)
        uv_t = jnp.concatenate([u[pr], v_f32], axis=0).T
        state_sc[pr] = (state[pr] + _mm3(_split(uv_t), bk_s[pr])) * w_end[:, sl]

    y = jnp.concatenate([m[:ell] + m[ell:] for m in y2], axis=1)
    inv_head = 1.0 / HEAD
    mean = head_sum(y) * inv_head
    yc = y - mean
    var = head_sum(yc * yc) * inv_head
    yn = yc * lax.rsqrt(var + GN_EPS) * gng_ref[...] + gnb_ref[...]
    bonus = head_sum(r * k2 * rk_ref[...]) * v
    y_ref[...] = ((yn + bonus) * g).astype(y_ref.dtype)

    @pl.when(c == n_chunks - 1)
    def _():
        last = (t_total - 1) % ell
        shift_ref[...] = p[last:last + 1, :]
        sout_ref[...] = state_sc[...]


def _pad_rows(w, rows, offset):
    out = jnp.zeros((rows, w.shape[1]), F32)
    return out.at[offset:offset + w.shape[0]].set(w.astype(F32))


def _rwkv(p, shift_prev, wkv0, lw):
    bsz, t_total, width = p.shape
    d = width - LORA_WA - LORA_G
    d //= 3
    n_pairs = d // LANES
    ell = RWKV_CHUNK
    n_chunks = -(-t_total // ell)
    t_pad = n_chunks * ell
    if t_pad != t_total:
        p = jnp.pad(p, ((0, 0), (0, t_pad - t_total), (0, 0)))
    s0 = wkv0.astype(F32).reshape(bsz, n_pairs, 2, HEAD, HEAD)
    zero = jnp.zeros_like(s0[:, :, 0])
    s0 = jnp.concatenate([jnp.concatenate([s0[:, :, 0], zero], axis=-1),
                          jnp.concatenate([zero, s0[:, :, 1]], axis=-1)], axis=-2)
    row = lambda z: z.astype(F32).reshape(1, -1)
    ones_blk = jnp.kron(jnp.eye(d // HEAD, dtype=F32), jnp.ones((HEAD, HEAD), F32)).astype(BF16)
    hi_lo = lambda w: jnp.stack(_split(w))
    consts = [row(lw["mu"]), row(lw["w0"]), hi_lo(_pad_rows(lw["w_lora_w"], LORA_WA, 0)), row(lw["a0"]),
              hi_lo(_pad_rows(lw["w_lora_a"], LORA_WA, LORA_WA // 2)), hi_lo(lw["w_lora_g"].astype(F32)),
              row(lw["k_k"]), row(lw["k_a"]), row(lw["r_k"]), row(lw["lnx_g"]), row(lw["lnx_b"]), ones_blk]
    const_specs = [pl.BlockSpec(cst.shape, lambda b, c, nd=cst.ndim: (0,) * nd) for cst in consts]
    y, shift, sout = pl.pallas_call(
        functools.partial(_rwkv_kernel, chunk=ell, t_total=t_total, n_chunks=n_chunks),
        grid=(bsz, n_chunks),
        in_specs=[pl.BlockSpec((None, ell, width), lambda b, c: (b, c, 0)),
                  pl.BlockSpec((None, 1, width), lambda b, c: (b, 0, 0)),
                  pl.BlockSpec((None, n_pairs, LANES, LANES), lambda b, c: (b, 0, 0, 0))] + const_specs,
        out_specs=[pl.BlockSpec((None, ell, d), lambda b, c: (b, c, 0)),
                   pl.BlockSpec((None, 1, width), lambda b, c: (b, 0, 0)),
                   pl.BlockSpec((None, n_pairs, LANES, LANES), lambda b, c: (b, 0, 0, 0))],
        out_shape=[jax.ShapeDtypeStruct((bsz, t_pad, d), BF16),
                   jax.ShapeDtypeStruct((bsz, 1, width), F32),
                   jax.ShapeDtypeStruct((bsz, n_pairs, LANES, LANES), F32)],
        scratch_shapes=[pltpu.VMEM((1, width), F32), pltpu.VMEM((n_pairs, LANES, LANES), F32)],
        compiler_params=_params("parallel", "arbitrary"),
        name="rwkv7_chunked",
    )(p, shift_prev.astype(F32).reshape(bsz, 1, width), s0, *consts)
    wkv = jnp.stack([sout[:, :, :HEAD, :HEAD], sout[:, :, HEAD:, HEAD:]], axis=2)
    return y[:, :t_total], shift[:, 0], wkv.reshape(bsz, 2 * n_pairs, HEAD, HEAD)


def _gelu_tanh(x):
    return 0.5 * x * (1.0 + jnp.tanh(math.sqrt(2.0 / math.pi) * (x + 0.044715 * (x * x * x))))


def _s5_kernel(u_ref, h0re_ref, h0im_ref, abre_ref, abim_ref, wbre_ref, wbim_ref, wcre_ref, wcim_ref, dskip_ref,
               wglu_ref, bglu_ref, y_ref, hre_ref, him_ref, ere_sc, eim_sc, hre_sc, him_sc,
               *, tb, n_blocks, blocks_per_group):
    tblk = pl.program_id(1)
    nb = SUBLANES
    n_state = hre_sc.shape[1]
    n_in = u_ref.shape[-1]
    n_panels = wbre_ref.shape[0]
    p_in, p_state = n_in // n_panels, n_state // n_panels

    @pl.when(tblk == 0)
    def _():
        hre_sc[...] = h0re_ref[...]
        him_sc[...] = h0im_ref[...]

    u = u_ref[...].reshape(tb * nb, n_in)
    ub = u.astype(BF16)
    for pn in range(n_panels):
        up = ub[:, pn * p_in:(pn + 1) * p_in]
        ere_sc[:, pn * p_state:(pn + 1) * p_state] = _dot(up, wbre_ref[pn])
        eim_sc[:, pn * p_state:(pn + 1) * p_state] = _dot(up, wbim_ref[pn])

    n_blk = n_state // LANES
    for cg in range(n_blk // blocks_per_group):
        blks = range(cg * blocks_per_group, (cg + 1) * blocks_per_group)
        a_re = [jnp.broadcast_to(abre_ref[:, jb * LANES:(jb + 1) * LANES], (nb, LANES)) for jb in blks]
        a_im = [jnp.broadcast_to(abim_ref[:, jb * LANES:(jb + 1) * LANES], (nb, LANES)) for jb in blks]

        def step(t, carry, blks=blks, a_re=a_re, a_im=a_im):
            rows = pl.ds(pl.multiple_of(t * nb, nb), nb)
            new_re, new_im = [], []
            for n, jb in enumerate(blks):
                cs = slice(jb * LANES, (jb + 1) * LANES)
                h_re, h_im = carry[0][n], carry[1][n]
                n_re = a_re[n] * h_re - a_im[n] * h_im + ere_sc[rows, cs]
                n_im = a_re[n] * h_im + a_im[n] * h_re + eim_sc[rows, cs]
                ere_sc[rows, cs] = n_re
                eim_sc[rows, cs] = n_im
                new_re.append(n_re)
                new_im.append(n_im)
            return tuple(new_re), tuple(new_im)

        init = (tuple(hre_sc[:, jb * LANES:(jb + 1) * LANES] for jb in blks),
                tuple(him_sc[:, jb * LANES:(jb + 1) * LANES] for jb in blks))
        fin_re, fin_im = lax.fori_loop(0, tb, step, init, unroll=2 if tb % 2 == 0 else 1)
        for n, jb in enumerate(blks):
            hre_sc[:, jb * LANES:(jb + 1) * LANES] = fin_re[n]
            him_sc[:, jb * LANES:(jb + 1) * LANES] = fin_im[n]

    y_parts = []
    for pn in range(n_panels):
        ss = slice(pn * p_state, (pn + 1) * p_state)
        y_parts.append(_dot(ere_sc[:, ss].astype(BF16), wcre_ref[pn]) - _dot(eim_sc[:, ss].astype(BF16), wcim_ref[pn]))
    y = jnp.concatenate(y_parts, axis=1) + dskip_ref[...] * u
    y = _gelu_tanh(y)
    y = y * jax.nn.sigmoid(_dot(y.astype(BF16), wglu_ref[...]) + bglu_ref[...])
    y_ref[...] = y.reshape(tb, nb, n_in).astype(y_ref.dtype)

    @pl.when(tblk == n_blocks - 1)
    def _():
        hre_ref[...] = hre_sc[...]
        him_ref[...] = him_sc[...]


def _s5_weights(lw):
    lr, li = lw["s5_lambda_re"].astype(F32), lw["s5_lambda_im"].astype(F32)
    n_groups = lr.shape[0]
    dt = jnp.exp(lw["s5_log_dt"].astype(F32))[:, None]
    mag = jnp.exp(lr * dt)
    ab_re, ab_im = mag * jnp.cos(li * dt), mag * jnp.sin(li * dt)
    den = lr * lr + li * li
    n_re = ab_re - 1.0
    f_re = (n_re * lr + ab_im * li) / den
    f_im = (ab_im * lr - n_re * li) / den
    br, bi = lw["s5_b_re"].astype(F32), lw["s5_b_im"].astype(F32)
    bb_re = f_re[..., None] * br - f_im[..., None] * bi
    bb_im = f_re[..., None] * bi + f_im[..., None] * br
    eye = jnp.eye(n_groups, dtype=F32)
    n_in, n_state = n_groups * S5_GROUP, n_groups * S5_STATE
    n_panels = max(1, n_in // (2 * LANES))
    p_in, p_state = n_in // n_panels, n_state // n_panels

    def panels(w, rows, cols):
        return jnp.stack([w[pn * rows:(pn + 1) * rows, pn * cols:(pn + 1) * cols] for pn in range(n_panels)])

    expand_b = lambda z: panels(jnp.einsum("gpc,gh->gchp", z, eye).reshape(n_in, n_state), p_in, p_state).astype(BF16)
    expand_c = lambda z: panels(jnp.einsum("gcp,gh->gphc", z.astype(F32), eye).reshape(n_state, n_in),
                                p_state, p_in).astype(BF16)
    return dict(ab_re=ab_re.reshape(1, n_state), ab_im=ab_im.reshape(1, n_state),
                wb_re=expand_b(bb_re), wb_im=expand_b(bb_im),
                wc_re=expand_c(lw["s5_c_re"]), wc_im=expand_c(lw["s5_c_im"]),
                d_skip=lw["s5_d"].astype(F32).reshape(1, n_in),
                w_glu=lw["s5_w_glu"].astype(BF16), b_glu=lw["s5_b_glu"].astype(F32).reshape(1, n_in))


def _s5(u, h_re0, h_im0, sw, tb):
    bsz, t_total, n_in = u.shape
    n_state = sw["ab_re"].shape[1]
    nb = SUBLANES
    assert bsz % nb == 0 and t_total % tb == 0
    n_blocks = t_total // tb
    blocks_per_group = math.gcd(n_state // LANES, 8)
    consts = [sw["ab_re"], sw["ab_im"], sw["wb_re"], sw["wb_im"], sw["wc_re"], sw["wc_im"], sw["d_skip"],
              sw["w_glu"], sw["b_glu"]]
    seq_block = pl.BlockSpec((tb, nb, n_in), lambda g, t: (t, g, 0))
    state_block = pl.BlockSpec((nb, n_state), lambda g, t: (g, 0))
    y, h_re, h_im = pl.pallas_call(
        functools.partial(_s5_kernel, tb=tb, n_blocks=n_blocks, blocks_per_group=blocks_per_group),
        grid=(bsz // nb, n_blocks),
        in_specs=[seq_block, state_block, state_block]
                 + [pl.BlockSpec(cst.shape, lambda g, t, nd=cst.ndim: (0,) * nd) for cst in consts],
        out_specs=[seq_block, state_block, state_block],
        out_shape=[jax.ShapeDtypeStruct((t_total, bsz, n_in), BF16),
                   jax.ShapeDtypeStruct((bsz, n_state), F32),
                   jax.ShapeDtypeStruct((bsz, n_state), F32)],
        scratch_shapes=[pltpu.VMEM((nb * tb, n_state), F32), pltpu.VMEM((nb * tb, n_state), F32),
                        pltpu.VMEM((nb, n_state), F32), pltpu.VMEM((nb, n_state), F32)],
        compiler_params=_params("parallel", "arbitrary"),
        name="s5_scan",
    )(jnp.swapaxes(u, 0, 1), h_re0.astype(F32).reshape(bsz, n_state), h_im0.astype(F32).reshape(bsz, n_state),
      *consts)
    shape = (bsz, n_state // S5_STATE, S5_STATE)
    return jnp.swapaxes(y, 0, 1), h_re.reshape(shape), h_im.reshape(shape)


def _log_sigmoid(z):
    return jnp.minimum(z, 0.0) - jnp.log1p(jnp.exp(-jnp.abs(z)))


def _fox_proj_kernel(x_ref, w_ref, wf_ref, bf_ref, q_ref, k_ref, v_ref, kb_ref, vb_ref, lf_ref, c_ref, carry_sc,
                     *, d, scale, tiles_per_seq):
    i = pl.program_id(0)
    tm = x_ref.shape[0]
    xb = x_ref[...].astype(BF16)
    acc = _dot(xb, w_ref[...])
    q_ref[...] = (acc[:, :d] * scale).astype(BF16)
    k = acc[:, d:2 * d]
    v = acc[:, 2 * d:3 * d]
    k_ref[...] = k
    v_ref[...] = v
    kb_ref[...] = k.astype(BF16)
    vb_ref[...] = v.astype(BF16)
    lf = _log_sigmoid(_dot(xb, wf_ref[...]) + bf_ref[...])
    lf_ref[...] = lf

    @pl.when(i % tiles_per_seq == 0)
    def _():
        carry_sc[...] = jnp.zeros_like(carry_sc)

    ti = lax.broadcasted_iota(jnp.int32, (tm, tm), 0)
    tj = lax.broadcasted_iota(jnp.int32, (tm, tm), 1)
    c = _dot((ti >= tj).astype(F32), lf, HI) + carry_sc[...]
    c_ref[...] = c
    carry_sc[...] = c[tm - 1:tm, :]


def _fox_proj(x, w_qkv, w_f, b_f, seq_len, tm):
    m, kdim = x.shape
    d = w_qkv.shape[1] // 3
    nh = w_f.shape[1]
    assert seq_len % tm == 0 or tm % seq_len == 0
    row = lambda n: pl.BlockSpec((tm, n), lambda i: (i, 0))
    return pl.pallas_call(
        functools.partial(_fox_proj_kernel, d=d, scale=HEAD ** -0.5 * LOG2E, tiles_per_seq=max(seq_len // tm, 1)),
        grid=(m // tm,),
        in_specs=[row(kdim), pl.BlockSpec(w_qkv.shape, lambda i: (0, 0)), pl.BlockSpec(w_f.shape, lambda i: (0, 0)),
                  pl.BlockSpec((1, nh), lambda i: (0, 0))],
        out_specs=[row(d), row(d), row(d), row(d), row(d), row(nh), row(nh)],
        out_shape=[jax.ShapeDtypeStruct((m, d), BF16), jax.ShapeDtypeStruct((m, d), F32),
                   jax.ShapeDtypeStruct((m, d), F32), jax.ShapeDtypeStruct((m, d), BF16),
                   jax.ShapeDtypeStruct((m, d), BF16), jax.ShapeDtypeStruct((m, nh), F32),
                   jax.ShapeDtypeStruct((m, nh), F32)],
        scratch_shapes=[pltpu.VMEM((1, nh), F32)],
        compiler_params=_params("arbitrary"),
        name="fox_proj",
    )(x, w_qkv, w_f, b_f)


def _fox_attn_kernel(q_ref, k_ref, v_ref, c_ref, o_ref, m_sc, l_sc, acc_sc, *, tq, rq):
    qi = pl.program_id(2)
    n_rb = tq // rq
    heads = LANES // HEAD
    lane = lax.broadcasted_iota(jnp.int32, (1, LANES), 1)
    q = q_ref[...]
    qm = [jnp.where((lane // HEAD) == h, q, jnp.zeros_like(q)) for h in range(heads)]
    m_sc[...] = jnp.full_like(m_sc, -jnp.inf)
    l_sc[...] = jnp.zeros_like(l_sc)
    acc_sc[...] = jnp.zeros_like(acc_sc)

    def chain(h, rb, start, width, masked):
        rows = slice(rb * rq, (rb + 1) * rq)
        kt = k_ref[pl.ds(start, width), :]
        vt = v_ref[pl.ds(start, width), :]
        s = _dot_nt(qm[h][rows], kt) - c_ref[h, :, pl.ds(start, width)]
        if masked:
            ri = lax.broadcasted_iota(jnp.int32, (rq, width), 0) + rb * rq
            ci = lax.broadcasted_iota(jnp.int32, (rq, width), 1)
            s = jnp.where(ci <= ri, s, -jnp.inf)
        m_prev = m_sc[h, rows, :]
        m_new = jnp.maximum(m_prev, jnp.max(s, axis=1, keepdims=True))
        alpha = jnp.exp2(m_prev - m_new)
        p = jnp.exp2(s - jnp.concatenate([m_new] * (width // LANES), axis=1))
        l_sc[h, rows, :] = alpha * l_sc[h, rows, :] + jnp.sum(p, axis=1, keepdims=True)
        acc_sc[h, rows, :] = alpha * acc_sc[h, rows, :] + _dot(p.astype(BF16), vt)
        m_sc[h, rows, :] = m_new

    def body(ki, carry):
        start = pl.multiple_of(ki * tq, tq)
        for h in range(heads):
            for rb in range(n_rb):
                chain(h, rb, start, tq, False)
        return carry

    lax.fori_loop(0, qi, body, 0)
    start = pl.multiple_of(qi * tq, tq)
    for h in range(heads):
        for rb in range(n_rb):
            chain(h, rb, start, (rb + 1) * rq, True)
    out = [acc_sc[h] / l_sc[h] for h in range(heads)]
    o_ref[...] = jnp.where(lane < HEAD, out[0], out[1]).astype(o_ref.dtype)


def _fox_attn(q, kb, vb, c, tq):
    bsz, t_total, d = q.shape
    n_pairs = d // LANES
    hp = LANES // HEAD
    return pl.pallas_call(
        functools.partial(_fox_attn_kernel, tq=tq, rq=tq),
        grid=(bsz, n_pairs, t_total // tq),
        in_specs=[pl.BlockSpec((None, tq, LANES), lambda b, h, i: (b, i, h)),
                  pl.BlockSpec((None, t_total, LANES), lambda b, h, i: (b, 0, h)),
                  pl.BlockSpec((None, t_total, LANES), lambda b, h, i: (b, 0, h)),
                  pl.BlockSpec((None, hp, 1, t_total), lambda b, h, i: (b, h, 0, 0))],
        out_specs=pl.BlockSpec((None, tq, LANES), lambda b, h, i: (b, i, h)),
        out_shape=jax.ShapeDtypeStruct((bsz, t_total, d), BF16),
        scratch_shapes=[pltpu.VMEM((hp, tq, LANES), F32)] * 3,
        compiler_params=_params("parallel", "parallel", "arbitrary"),
        name="fox_attn",
    )(q, kb, vb, c)


def _fox_decode_kernel(pt_ref, q_ref, knew_ref, vnew_ref, cnew_ref, *refs, pages_per_step, n_steps):
    del pt_ref
    npp = pages_per_step
    k_refs, v_refs, lf_refs = refs[:npp], refs[npp:2 * npp], refs[2 * npp:3 * npp]
    o_ref, m_sc, l_sc, acc_sc, carry_sc, sb_sc, s_sc = refs[3 * npp:]
    j = pl.program_id(1)
    nh, hd = q_ref.shape
    page = k_refs[0].shape[0]

    @pl.when(j == 0)
    def _():
        m_sc[...] = jnp.full_like(m_sc, -jnp.inf)
        l_sc[...] = jnp.zeros_like(l_sc)
        acc_sc[...] = jnp.zeros_like(acc_sc)
        carry_sc[...] = cnew_ref[...]

    q3 = q_ref[...]
    own =(lax.broadcasted_iota(jnp.int32, (nh, LANES), 1)
           == lax.broadcasted_iota(jnp.int32, (nh, LANES), 0) + hd)
    upper = (lax.broadcasted_iota(jnp.int32, (page, page), 0)
             < lax.broadcasted_iota(jnp.int32, (page, page), 1)).astype(F32)
    shift = own.astype(F32)
    pad = jnp.zeros((nh, LANES - hd), F32)
    for i in range(npp):
        lfs = _dot(lf_refs[i][...] * LOG2E, shift, HI)
        sb_sc[...] = _dot(upper, lfs, HI) + carry_sc[...]
        carry_sc[...] = carry_sc[...] + jnp.sum(lfs, axis=0, keepdims=True)

        def score_row(r, carry, i=i):
            bias = jnp.where(own, jnp.broadcast_to(sb_sc[pl.ds(r, 1), :], (nh, LANES)), 0.0)
            z = jnp.concatenate([k_refs[i][r] * q3, pad], axis=1) + bias
            s_sc[r] = jnp.broadcast_to(jnp.sum(z, axis=1, keepdims=True), (nh, LANES))
            return carry

        lax.fori_loop(0, page, score_row, 0, unroll=True)
        s = s_sc[...]
        m_prev = m_sc[...]
        m_new = jnp.maximum(m_prev, jnp.max(s, axis=0))
        alpha = jnp.exp2(m_prev - m_new)
        p = jnp.exp2(s - m_new[None])
        l_sc[...] = alpha * l_sc[...] + jnp.sum(p, axis=0)
        acc_sc[...] = alpha[:, :hd] * acc_sc[...] + jnp.sum(p[:, :, :hd] * v_refs[i][...], axis=0)
        m_sc[...] = m_new

    @pl.when(j == n_steps - 1)
    def _():
        s_new = jnp.sum(q3 * knew_ref[...], axis=1, keepdims=True)
        m_prev = m_sc[...]
        m_new = jnp.maximum(m_prev, s_new)
        alpha = jnp.exp2(m_prev - m_new)
        p_new = jnp.exp2(s_new - m_new)
        l_fin = alpha * l_sc[...] + p_new
        o_ref[...] = (alpha[:, :hd] * acc_sc[...] + p_new[:, :hd] * vnew_ref[...]) / l_fin[:, :hd]


def _fox_decode(q, k_new, v_new, lf_new, cache_k, cache_v, cache_lf, page_table, layer, pages_per_step):
    bsz, d = q.shape
    nh = d // HEAD
    n_pages = page_table.shape[1]
    page = cache_k.shape[2]
    npp = pages_per_step
    assert n_pages % npp == 0
    n_steps = n_pages // npp
    pt = page_table.reshape(-1).astype(jnp.int32)
    c_new = jnp.pad(lf_new.astype(F32) * LOG2E, ((0, 0), (HEAD, LANES - HEAD - nh))).reshape(bsz, 1, LANES)

    def page_map(i, ndim):
        return lambda b, j, pt_ref: (layer, pt_ref[b * n_pages + (n_pages - 1 - (j * npp + i))]) + (0,) * ndim

    heads3 = lambda z: z.reshape(bsz, nh, HEAD)
    per_seq = lambda n0, n1: pl.BlockSpec((None, n0, n1), lambda b, j, pt_ref: (b, 0, 0))
    in_specs = [per_seq(nh, HEAD), per_seq(nh, HEAD), per_seq(nh, HEAD), per_seq(1, LANES)]
    in_specs += [pl.BlockSpec((None, None, page, nh, HEAD), page_map(i, 3)) for i in range(npp)]
    in_specs += [pl.BlockSpec((None, None, page, nh, HEAD), page_map(i, 3)) for i in range(npp)]
    in_specs += [pl.BlockSpec((None, None, page, nh), page_map(i, 2)) for i in range(npp)]
    out = pl.pallas_call(
        functools.partial(_fox_decode_kernel, pages_per_step=npp, n_steps=n_steps),
        grid_spec=pltpu.PrefetchScalarGridSpec(
            num_scalar_prefetch=1,
            grid=(bsz, n_steps),
            in_specs=in_specs,
            out_specs=per_seq(nh, HEAD),
            scratch_shapes=[pltpu.VMEM((nh, LANES), F32), pltpu.VMEM((nh, LANES), F32), pltpu.VMEM((nh, HEAD), F32),
                            pltpu.VMEM((1, LANES), F32), pltpu.VMEM((page, LANES), F32),
                            pltpu.VMEM((page, nh, LANES), F32)]),
        out_shape=jax.ShapeDtypeStruct((bsz, nh, HEAD), F32),
        compiler_params=_params("parallel", "arbitrary"),
        name="fox_decode",
    )(pt, heads3(q), heads3(k_new), heads3(v_new), c_new,
      *([cache_k] * npp), *([cache_v] * npp), *([cache_lf] * npp))
    return out.reshape(bsz, d)


def _mixer_ab(x2d, bsz, seq, shift_prev, wkv0, h_re0, h_im0, lw, tm, s5_tb):
    d_rwkv = lw["mu"].shape[0]
    d_s5 = lw["w_in"].shape[1] - d_rwkv
    p, u = _proj(x2d, lw["w_in"], (d_rwkv, d_s5), (F32, F32), tm)
    y_a, shift, wkv = _rwkv(p.reshape(bsz, seq, d_rwkv), shift_prev, wkv0, lw)
    y_b, h_re, h_im = _s5(u.reshape(bsz, seq, d_s5), h_re0, h_im0, lw["s5"], s5_tb)
    return (y_a.reshape(bsz * seq, -1), y_b.reshape(bsz * seq, -1)), shift, wkv, h_re, h_im


def kernel(x_prompt, x_sample, state_shift, state_wkv, state_s5_re, state_s5_im, cache_k, cache_v, cache_logf, page_table, w_in_ab, mu_shift, w0, w_lora_w, a0, w_lora_a, w_lora_g, k_k, k_a, r_k, lnx_g, lnx_b, s5_lambda_re, s5_lambda_im, s5_log_dt, s5_b_re, s5_b_im, s5_c_re, s5_c_im, s5_d, s5_w_glu, s5_b_glu, w_out_ab, w_in_c, b_forget, w_out_c, ln1_g, ln1_b, ln2_g, ln2_b, ffn_w_gate, ffn_w_up, ffn_w_down):
    bp, seq_p, d_model = x_prompt.shape
    bs, seq_s, _ = x_sample.shape
    assert seq_s == 1
    depth = ln1_g.shape[0]
    alpha = (2.0 * depth) ** 0.25
    n_heads = d_model // HEAD
    mp, ms = bp * seq_p, bs * seq_s
    tm_p, tm_s = _row_tile(mp, 512), _row_tile(ms, 512)
    tm_ffn = _row_tile(mp, 512)
    row = lambda z: z.astype(F32).reshape(1, -1)

    yp = x_prompt.reshape(mp, d_model)
    ys = x_sample.reshape(ms, d_model)

    shift_p, shift_s, wkv_p, wkv_s = [], [], [], []
    s5re_p, s5re_s, s5im_p, s5im_s = [], [], [], []
    k_p, k_s, v_p, v_s, lf_p, lf_s = [], [], [], [], [], []
    for layer in range(depth):
        j = layer // 2
        if layer % 2 == 0:
            lw = dict(w_in=w_in_ab[j].astype(BF16), mu=mu_shift[j], w0=w0[j], w_lora_w=w_lora_w[j], a0=a0[j],
                      w_lora_a=w_lora_a[j], w_lora_g=w_lora_g[j], k_k=k_k[j], k_a=k_a[j], r_k=r_k[j],
                      lnx_g=lnx_g[j], lnx_b=lnx_b[j], s5_lambda_re=s5_lambda_re[j], s5_lambda_im=s5_lambda_im[j],
                      s5_log_dt=s5_log_dt[j], s5_b_re=s5_b_re[j], s5_b_im=s5_b_im[j], s5_c_re=s5_c_re[j],
                      s5_c_im=s5_c_im[j], s5_d=s5_d[j], s5_w_glu=s5_w_glu[j], s5_b_glu=s5_b_glu[j])
            lw["s5"] = _s5_weights(lw)
            w_out = w_out_ab[j].astype(BF16)
            n_s5 = lw["s5"]["ab_re"].shape[1]
            zeros = lambda *shape: jnp.zeros(shape, F32)
            parts_p, sh, wk, hr, hi = _mixer_ab(
                yp, bp, seq_p, zeros(bp, mu_shift.shape[1]), zeros(bp, (w0.shape[1]) // HEAD, HEAD, HEAD),
                zeros(bp, n_s5), zeros(bp, n_s5), lw, tm_p, min(seq_p, 128))
            shift_p.append(sh); wkv_p.append(wk); s5re_p.append(hr); s5im_p.append(hi)
            parts_s, sh, wk, hr, hi = _mixer_ab(
                ys, bs, seq_s, state_shift[j], state_wkv[j], state_s5_re[j], state_s5_im[j], lw, tm_s, 1)
            shift_s.append(sh); wkv_s.append(wk); s5re_s.append(hr); s5im_s.append(hi)
        else:
            w_c = w_in_c[j].astype(BF16)
            w_qkv, w_f = w_c[:, :3 * d_model], w_c[:, 3 * d_model:]
            b_f = row(b_forget[j])
            w_out = w_out_c[j].astype(BF16)
            q, k, v, kb, vb, lf, c = _fox_proj(yp, w_qkv, w_f, b_f, seq_p, tm_p)
            c = jnp.swapaxes(c.reshape(bp, seq_p, n_heads) * LOG2E, 1, 2).reshape(bp, n_heads, 1, seq_p)
            o = _fox_attn(q.reshape(bp, seq_p, d_model), kb.reshape(bp, seq_p, d_model),
                          vb.reshape(bp, seq_p, d_model), c, min(seq_p, 512))
            parts_p = (o.reshape(mp, d_model),)
            k_p.append(k.reshape(bp, seq_p, n_heads, HEAD)); v_p.append(v.reshape(bp, seq_p, n_heads, HEAD))
            lf_p.append(lf.reshape(bp, seq_p, n_heads))
            q, k, v, _, _, lf, _ = _fox_proj(ys, w_qkv, w_f, b_f, seq_s, tm_s)
            o = _fox_decode(q.astype(F32), k, v, lf, cache_k, cache_v, cache_logf, page_table, j, 4)
            parts_s = (o,)
            k_s.append(k.reshape(bs, seq_s, n_heads, HEAD)); v_s.append(v.reshape(bs, seq_s, n_heads, HEAD))
            lf_s.append(lf.reshape(bs, seq_s, n_heads))
        g1, b1, g2, b2 = row(ln1_g[layer]), row(ln1_b[layer]), row(ln2_g[layer]), row(ln2_b[layer])
        wg, wu, wd = ffn_w_gate[layer].astype(BF16), ffn_w_up[layer].astype(BF16), ffn_w_down[layer].astype(BF16)
        yp = _outproj_ln(parts_p, w_out, yp, g1, b1, alpha, tm_p)
        ys = _outproj_ln(parts_s, w_out, ys, g1, b1, alpha, tm_s)
        yp = _ffn_ln(yp, wg, wu, wd, g2, b2, alpha, tm_ffn)
        ys = _ffn_ln(ys, wg, wu, wd, g2, b2, alpha, tm_s)
    return (yp.reshape(bp, seq_p, d_model), ys.reshape(bs, seq_s, d_model),
            jnp.stack(shift_p), jnp.stack(shift_s), jnp.stack(wkv_p), jnp.stack(wkv_s),
            jnp.stack(s5re_p), jnp.stack(s5re_s), jnp.stack(s5im_p), jnp.stack(s5im_s),
            jnp.stack(k_p), jnp.stack(k_s), jnp.stack(v_p), jnp.stack(v_s),
            jnp.stack(lf_p), jnp.stack(lf_s))
```

```python
import functools
import math

import jax
import jax.numpy as jnp
from jax import lax
from jax.experimental import pallas as pl
from jax.experimental.pallas import tpu as pltpu

F32, BF16 = jnp.float32, jnp.bfloat16
HI = lax.Precision.HIGHEST

LANES = 128
SUBLANES = 8
VMEM_LIMIT_BYTES = 56 * 1024 * 1024

HEAD = 64
S5_GROUP = 16
S5_STATE = 64
LORA_WA = 128
LORA_G = 128
GN_EPS = 64e-5
LN_EPS = 1e-5
RWKV_CHUNK = 64
LOG2E = math.log2(math.e)


def _dot(a, b, precision=None):
    return jnp.dot(a, b, preferred_element_type=F32, precision=precision)


def _dot_nt(a, b, precision=None):
    return lax.dot_general(a, b, (((1,), (1,)), ((), ())), preferred_element_type=F32, precision=precision)


def _dot_tn(a, b, precision=None):
    return lax.dot_general(a, b, (((0,), (0,)), ((), ())), preferred_element_type=F32, precision=precision)


def _split(a, terms=2):
    out = []
    for _ in range(terms - 1):
        piece = a.astype(BF16)
        out.append(piece)
        a = a - piece.astype(F32)
    out.append(a.astype(BF16))
    return tuple(out)


def _mm3(a2, b2, dot=_dot):
    (a_hi, a_lo), (b_hi, b_lo) = a2, b2
    return dot(a_hi, b_hi) + dot(a_hi, b_lo) + dot(a_lo, b_hi)


def _mm_exact_rhs(a_terms, b):
    acc = None
    for piece in a_terms:
        t = _dot(piece, b)
        acc = t if acc is None else acc + t
    return acc


def _params(*semantics):
    return pltpu.CompilerParams(dimension_semantics=semantics, vmem_limit_bytes=VMEM_LIMIT_BYTES)


def _row_tile(m, preferred):
    tm = min(m, preferred)
    assert m % tm == 0, (m, tm)
    return tm


def _layer_norm(z, g, b):
    mu = jnp.mean(z, axis=-1, keepdims=True)
    zc = z - mu
    var = jnp.mean(zc * zc, axis=-1, keepdims=True)
    return zc * lax.rsqrt(var + LN_EPS) * g + b


def _softplus(z):
    return jnp.maximum(z, 0.0) + jnp.log1p(jnp.exp(-jnp.abs(z)))


def _proj_kernel(x_ref, w_ref, *o_refs, widths):
    acc = _dot(x_ref[...].astype(BF16), w_ref[...])
    off = 0
    for o_ref, n in zip(o_refs, widths):
        o_ref[...] = acc[:, off:off + n].astype(o_ref.dtype)
        off += n


def _proj(x, w, widths, dtypes, tm):
    m, k = x.shape
    n = w.shape[1]
    assert sum(widths) == n
    return pl.pallas_call(
        functools.partial(_proj_kernel, widths=tuple(widths)),
        grid=(m // tm,),
        in_specs=[pl.BlockSpec((tm, k), lambda i: (i, 0)), pl.BlockSpec((k, n), lambda i: (0, 0))],
        out_specs=[pl.BlockSpec((tm, wd), lambda i: (i, 0)) for wd in widths],
        out_shape=[jax.ShapeDtypeStruct((m, wd), dt) for wd, dt in zip(widths, dtypes)],
        compiler_params=_params("parallel"),
        name="proj_split",
    )(x, w)


def _outproj_ln_kernel(*refs, n_parts, alpha):
    part_refs = refs[:n_parts]
    w_ref, x_ref, g_ref, b_ref, o_ref = refs[n_parts:]
    off = 0
    mix = None
    for p_ref in part_refs:
        kp = p_ref.shape[1]
        t = _dot(p_ref[...].astype(BF16), w_ref[off:off + kp, :])
        mix = t if mix is None else mix + t
        off += kp
    o_ref[...] = _layer_norm(alpha * x_ref[...] + mix, g_ref[...], b_ref[...])


def _outproj_ln(parts, w, x, g, b, alpha, tm):
    m, d = x.shape
    in_specs = [pl.BlockSpec((tm, p.shape[1]), lambda i: (i, 0)) for p in parts]
    in_specs += [pl.BlockSpec(w.shape, lambda i: (0, 0)), pl.BlockSpec((tm, d), lambda i: (i, 0)),
                 pl.BlockSpec((1, d), lambda i: (0, 0)), pl.BlockSpec((1, d), lambda i: (0, 0))]
    return pl.pallas_call(
        functools.partial(_outproj_ln_kernel, n_parts=len(parts), alpha=alpha),
        grid=(m // tm,),
        in_specs=in_specs,
        out_specs=pl.BlockSpec((tm, d), lambda i: (i, 0)),
        out_shape=jax.ShapeDtypeStruct((m, d), F32),
        compiler_params=_params("parallel"),
        name="outproj_ln",
    )(*parts, w, x, g, b)


def _ffn_chunk(dff):
    for c in (512, 256, 128):
        if dff % c == 0:
            return c
    return dff


def _ffn_ln_kernel(x_ref, wg_ref, wu_ref, wd_ref, g_ref, b_ref, o_ref, *, alpha, chunk):
    x = x_ref[...]
    xb = x.astype(BF16)
    dff = wg_ref.shape[1]
    out = None
    for c in range(dff // chunk):
        cs = slice(c * chunk, (c + 1) * chunk)
        gate = _dot(xb, wg_ref[:, cs])
        up = _dot(xb, wu_ref[:, cs])
        h = (gate * jax.nn.sigmoid(gate) * up).astype(BF16)
        t = _dot(h, wd_ref[cs, :])
        out = t if out is None else out + t
    o_ref[...] = _layer_norm(alpha * x + out, g_ref[...], b_ref[...])


def _ffn_ln(x, wg, wu, wd, g, b, alpha, tm):
    m, d = x.shape
    dff = wg.shape[1]
    const = lambda i: (0, 0)
    return pl.pallas_call(
        functools.partial(_ffn_ln_kernel, alpha=alpha, chunk=_ffn_chunk(dff)),
        grid=(m // tm,),
        in_specs=[pl.BlockSpec((tm, d), lambda i: (i, 0)), pl.BlockSpec((d, dff), const),
                  pl.BlockSpec((d, dff), const), pl.BlockSpec((dff, d), const),
                  pl.BlockSpec((1, d), const), pl.BlockSpec((1, d), const)],
        out_specs=pl.BlockSpec((tm, d), lambda i: (i, 0)),
        out_shape=jax.ShapeDtypeStruct((m, d), F32),
        compiler_params=_params("parallel"),
        name="ffn_ln",
    )(x, wg, wu, wd, g, b)


def _rwkv_kernel(p_ref, sprev_ref, s0_ref, mu_ref, w0_ref, wlw_ref, a0_ref, wla_ref, wlg_ref, kk_ref, ka_ref,
                 rk_ref, gng_ref, gnb_ref, ones_ref, y_ref, shift_ref, sout_ref, carry_sc, state_sc,
                 *, chunk, t_total, n_chunks):
    c = pl.program_id(1)
    d = y_ref.shape[-1]
    n_pairs = d // LANES
    ell = chunk

    @pl.when(c == 0)
    def _():
        carry_sc[...] = sprev_ref[...]
        state_sc[...] = s0_ref[...]

    p = p_ref[...]
    row = lax.broadcasted_iota(jnp.int32, (ell, 1), 0)
    prev = jnp.where(row == 0, carry_sc[...], pltpu.roll(p, 1, 0))
    ps = p + (prev - p) * mu_ref[...]
    carry_sc[...] = p[ell - 1:ell, :]

    r = ps[:, 0:d]
    k = ps[:, d:2 * d]
    v = ps[:, 2 * d:3 * d]
    x_wa = ps[:, 3 * d:3 * d + LORA_WA]
    x_g = ps[:, 3 * d + LORA_WA:3 * d + LORA_WA + LORA_G]
    ones_blk = ones_ref[...]
    head_sum = lambda z: _mm_exact_rhs(_split(z), ones_blk)

    w_log = -_softplus(-(w0_ref[...] + _mm3(_split(jnp.tanh(x_wa)), (wlw_ref[0], wlw_ref[1])))) - 0.5
    logw = -jnp.exp(w_log)
    a = jax.nn.sigmoid(a0_ref[...] + _mm3(_split(x_wa), (wla_ref[0], wla_ref[1])))
    g = _mm3(_split(jax.nn.sigmoid(x_g)), (wlg_ref[0], wlg_ref[1]))
    kk = k * kk_ref[...]
    kk = kk / jnp.maximum(jnp.sqrt(head_sum(kk * kk)), 1e-12)
    k2 = k * (1.0 + (a - 1.0) * ka_ref[...])
    bv = kk * a
    if t_total % ell:
        valid = (c * ell + row) < t_total
        logw = jnp.where(valid, logw, 0.0)
        kk = jnp.where(valid, kk, 0.0)
        bv = jnp.where(valid, bv, 0.0)
        k2 = jnp.where(valid, k2, 0.0)

    ti = lax.broadcasted_iota(jnp.int32, (ell, ell), 0)
    tj = lax.broadcasted_iota(jnp.int32, (ell, ell), 1)
    tri = (ti >= tj).astype(BF16)
    cl = None
    for piece in _split(logw, 3):
        t = _dot(tri, piece)
        cl = t if cl is None else cl + t
    e_in = jnp.exp(cl)
    e_neg = jnp.exp(-cl)
    a_t = _split(-kk * jnp.exp(cl - logw))
    r_t = _split(r * e_in)
    b_t = _split(bv * e_neg)
    k_t = _split(k2 * e_neg)
    v_t = _split(v)
    w_end = e_in[ell - 1:ell, :]

    lane = lax.broadcasted_iota(jnp.int32, (1, LANES), 1)
    first = lane < HEAD
    si = lax.broadcasted_iota(jnp.int32, (2 * ell, 2 * ell), 0) % ell
    sj = lax.broadcasted_iota(jnp.int32, (2 * ell, 2 * ell), 1) % ell
    strict = si > sj
    incl = si >= sj
    eye = (lax.broadcasted_iota(jnp.int32, (2 * ell, 2 * ell), 0)
           == lax.broadcasted_iota(jnp.int32, (2 * ell, 2 * ell), 1)).astype(F32)

    def stack(z2, sl):
        zero = jnp.zeros((ell, LANES), BF16)
        return tuple(jnp.concatenate([jnp.where(first, z[:, sl], zero), jnp.where(first, zero, z[:, sl])], axis=0)
                     for z in z2)

    cat = lambda x2, y2: tuple(jnp.concatenate([x, y], axis=0) for x, y in zip(x2, y2))

    pairs = range(n_pairs)
    sls = [slice(pr * LANES, (pr + 1) * LANES) for pr in pairs]
    a_s, r_s, v_s = [stack(a_t, sl) for sl in sls], [stack(r_t, sl) for sl in sls], [stack(v_t, sl) for sl in sls]
    bk_s = [cat(stack(b_t, sl), stack(k_t, sl)) for sl in sls]
    state = [state_sc[pr] for pr in pairs]
    state2 = [_split(s) for s in state]
    gram = [_mm3(cat(a_s[pr], r_s[pr]), bk_s[pr], _dot_nt) for pr in pairs]
    n_ab = [jnp.where(strict, g[:2 * ell, :2 * ell], 0.0) for g in gram]
    n_ak = [jnp.where(strict, g[:2 * ell, 2 * ell:], 0.0) for g in gram]
    n_rb = [jnp.where(incl, g[2 * ell:, :2 * ell], 0.0) for g in gram]
    n_rk = [jnp.where(incl, g[2 * ell:, 2 * ell:], 0.0) for g in gram]
    inv = [eye + n for n in n_ab]
    pw = list(n_ab)
    for _ in range(int(math.log2(ell)) - 1):
        pw2 = [_split(m) for m in pw]
        pw = [_mm3(m2, m2) for m2 in pw2]
        inv = [inv[pr] + _mm3(_split(pw[pr]), _split(inv[pr])) for pr in pairs]
    x = [_mm3(a_s[pr], state2[pr], _dot_nt) + _mm3(_split(n_ak[pr]), v_s[pr]) for pr in pairs]
    u = [_mm3(_split(inv[pr]), _split(x[pr])) for pr in pairs]
    y2 = [_mm3(r_s[pr], state2[pr], _dot_nt) + _mm3(_split(n_rb[pr]), _split(u[pr])) + _mm3(_split(n_rk[pr]), v_s[pr])
          for pr in pairs]
    for pr in pairs:
        sl = sls[pr]
        v_f32 = jnp.concatenate([jnp.where(first, v[:, sl], 0.0), jnp.where(first, 0.0, v[:, sl])], axis=0)
        uv_t = jnp.concatenate([u[pr], v_f32], axis=0).T
        state_sc[pr] = (state[pr] + _mm3(_split(uv_t), bk_s[pr])) * w_end[:, sl]

    y = jnp.concatenate([m[:ell] + m[ell:] for m in y2], axis=1)
    inv_head = 1.0 / HEAD
    mean = head_sum(y) * inv_head
    yc = y - mean
    var = head_sum(yc * yc) * inv_head
    yn = yc * lax.rsqrt(var + GN_EPS) * gng_ref[...] + gnb_ref[...]
    bonus = head_sum(r * k2 * rk_ref[...]) * v
    y_ref[...] = ((yn + bonus) * g).astype(y_ref.dtype)

    @pl.when(c == n_chunks - 1)
    def _():
        last = (t_total - 1) % ell
        shift_ref[...] = p[last:last + 1, :]
        sout_ref[...] = state_sc[...]


def _pad_rows(w, rows, offset):
    out = jnp.zeros((rows, w.shape[1]), F32)
    return out.at[offset:offset + w.shape[0]].set(w.astype(F32))


def _rwkv(p, shift_prev, wkv0, lw):
    bsz, t_total, width = p.shape
    d = width - LORA_WA - LORA_G
    d //= 3
    n_pairs = d // LANES
    ell = RWKV_CHUNK
    n_chunks = -(-t_total // ell)
    t_pad = n_chunks * ell
    if t_pad != t_total:
        p = jnp.pad(p, ((0, 0), (0, t_pad - t_total), (0, 0)))
    s0 = wkv0.astype(F32).reshape(bsz, n_pairs, 2, HEAD, HEAD)
    zero = jnp.zeros_like(s0[:, :, 0])
    s0 = jnp.concatenate([jnp.concatenate([s0[:, :, 0], zero], axis=-1),
                          jnp.concatenate([zero, s0[:, :, 1]], axis=-1)], axis=-2)
    row = lambda z: z.astype(F32).reshape(1, -1)
    ones_blk = jnp.kron(jnp.eye(d // HEAD, dtype=F32), jnp.ones((HEAD, HEAD), F32)).astype(BF16)
    hi_lo = lambda w: jnp.stack(_split(w))
    consts = [row(lw["mu"]), row(lw["w0"]), hi_lo(_pad_rows(lw["w_lora_w"], LORA_WA, 0)), row(lw["a0"]),
              hi_lo(_pad_rows(lw["w_lora_a"], LORA_WA, LORA_WA // 2)), hi_lo(lw["w_lora_g"].astype(F32)),
              row(lw["k_k"]), row(lw["k_a"]), row(lw["r_k"]), row(lw["lnx_g"]), row(lw["lnx_b"]), ones_blk]
    const_specs = [pl.BlockSpec(cst.shape, lambda b, c, nd=cst.ndim: (0,) * nd) for cst in consts]
    y, shift, sout = pl.pallas_call(
        functools.partial(_rwkv_kernel, chunk=ell, t_total=t_total, n_chunks=n_chunks),
        grid=(bsz, n_chunks),
        in_specs=[pl.BlockSpec((None, ell, width), lambda b, c: (b, c, 0)),
                  pl.BlockSpec((None, 1, width), lambda b, c: (b, 0, 0)),
                  pl.BlockSpec((None, n_pairs, LANES, LANES), lambda b, c: (b, 0, 0, 0))] + const_specs,
        out_specs=[pl.BlockSpec((None, ell, d), lambda b, c: (b, c, 0)),
                   pl.BlockSpec((None, 1, width), lambda b, c: (b, 0, 0)),
                   pl.BlockSpec((None, n_pairs, LANES, LANES), lambda b, c: (b, 0, 0, 0))],
        out_shape=[jax.ShapeDtypeStruct((bsz, t_pad, d), BF16),
                   jax.ShapeDtypeStruct((bsz, 1, width), F32),
                   jax.ShapeDtypeStruct((bsz, n_pairs, LANES, LANES), F32)],
        scratch_shapes=[pltpu.VMEM((1, width), F32), pltpu.VMEM((n_pairs, LANES, LANES), F32)],
        compiler_params=_params("parallel", "arbitrary"),
        name="rwkv7_chunked",
    )(p, shift_prev.astype(F32).reshape(bsz, 1, width), s0, *consts)
    wkv = jnp.stack([sout[:, :, :HEAD, :HEAD], sout[:, :, HEAD:, HEAD:]], axis=2)
    return y[:, :t_total], shift[:, 0], wkv.reshape(bsz, 2 * n_pairs, HEAD, HEAD)


def _gelu_tanh(x):
    return 0.5 * x * (1.0 + jnp.tanh(math.sqrt(2.0 / math.pi) * (x + 0.044715 * (x * x * x))))


def _s5_kernel(u_ref, h0re_ref, h0im_ref, abre_ref, abim_ref, wbre_ref, wbim_ref, wcre_ref, wcim_ref, dskip_ref,
               wglu_ref, bglu_ref, y_ref, hre_ref, him_ref, ere_sc, eim_sc, hre_sc, him_sc,
               *, tb, n_blocks, blocks_per_group):
    tblk = pl.program_id(1)
    nb = SUBLANES
    n_state = hre_sc.shape[1]
    n_in = u_ref.shape[-1]
    n_panels = wbre_ref.shape[0]
    p_in, p_state = n_in // n_panels, n_state // n_panels

    @pl.when(tblk == 0)
    def _():
        hre_sc[...] = h0re_ref[...]
        him_sc[...] = h0im_ref[...]

    u = u_ref[...].reshape(tb * nb, n_in)
    ub = u.astype(BF16)
    for pn in range(n_panels):
        up = ub[:, pn * p_in:(pn + 1) * p_in]
        ere_sc[:, pn * p_state:(pn + 1) * p_state] = _dot(up, wbre_ref[pn])
        eim_sc[:, pn * p_state:(pn + 1) * p_state] = _dot(up, wbim_ref[pn])

    n_blk = n_state // LANES
    for cg in range(n_blk // blocks_per_group):
        blks = range(cg * blocks_per_group, (cg + 1) * blocks_per_group)
        a_re = [jnp.broadcast_to(abre_ref[:, jb * LANES:(jb + 1) * LANES], (nb, LANES)) for jb in blks]
        a_im = [jnp.broadcast_to(abim_ref[:, jb * LANES:(jb + 1) * LANES], (nb, LANES)) for jb in blks]

        def step(t, carry, blks=blks, a_re=a_re, a_im=a_im):
            rows = pl.ds(pl.multiple_of(t * nb, nb), nb)
            new_re, new_im = [], []
            for n, jb in enumerate(blks):
                cs = slice(jb * LANES, (jb + 1) * LANES)
                h_re, h_im = carry[0][n], carry[1][n]
                n_re = a_re[n] * h_re - a_im[n] * h_im + ere_sc[rows, cs]
                n_im = a_re[n] * h_im + a_im[n] * h_re + eim_sc[rows, cs]
                ere_sc[rows, cs] = n_re
                eim_sc[rows, cs] = n_im
                new_re.append(n_re)
                new_im.append(n_im)
            return tuple(new_re), tuple(new_im)

        init = (tuple(hre_sc[:, jb * LANES:(jb + 1) * LANES] for jb in blks),
                tuple(him_sc[:, jb * LANES:(jb + 1) * LANES] for jb in blks))
        fin_re, fin_im = lax.fori_loop(0, tb, step, init, unroll=2 if tb % 2 == 0 else 1)
        for n, jb in enumerate(blks):
            hre_sc[:, jb * LANES:(jb + 1) * LANES] = fin_re[n]
            him_sc[:, jb * LANES:(jb + 1) * LANES] = fin_im[n]

    y_parts = []
    for pn in range(n_panels):
        ss = slice(pn * p_state, (pn + 1) * p_state)
        y_parts.append(_dot(ere_sc[:, ss].astype(BF16), wcre_ref[pn]) - _dot(eim_sc[:, ss].astype(BF16), wcim_ref[pn]))
    y = jnp.concatenate(y_parts, axis=1) + dskip_ref[...] * u
    y = _gelu_tanh(y)
    y = y * jax.nn.sigmoid(_dot(y.astype(BF16), wglu_ref[...]) + bglu_ref[...])
    y_ref[...] = y.reshape(tb, nb, n_in).astype(y_ref.dtype)

    @pl.when(tblk == n_blocks - 1)
    def _():
        hre_ref[...] = hre_sc[...]
        him_ref[...] = him_sc[...]


def _s5_weights(lw):
    lr, li = lw["s5_lambda_re"].astype(F32), lw["s5_lambda_im"].astype(F32)
    n_groups = lr.shape[0]
    dt = jnp.exp(lw["s5_log_dt"].astype(F32))[:, None]
    mag = jnp.exp(lr * dt)
    ab_re, ab_im = mag * jnp.cos(li * dt), mag * jnp.sin(li * dt)
    den = lr * lr + li * li
    n_re = ab_re - 1.0
    f_re = (n_re * lr + ab_im * li) / den
    f_im = (ab_im * lr - n_re * li) / den
    br, bi = lw["s5_b_re"].astype(F32), lw["s5_b_im"].astype(F32)
    bb_re = f_re[..., None] * br - f_im[..., None] * bi
    bb_im = f_re[..., None] * bi + f_im[..., None] * br
    eye = jnp.eye(n_groups, dtype=F32)
    n_in, n_state = n_groups * S5_GROUP, n_groups * S5_STATE
    n_panels = max(1, n_in // (2 * LANES))
    p_in, p_state = n_in // n_panels, n_state // n_panels

    def panels(w, rows, cols):
        return jnp.stack([w[pn * rows:(pn + 1) * rows, pn * cols:(pn + 1) * cols] for pn in range(n_panels)])

    expand_b = lambda z: panels(jnp.einsum("gpc,gh->gchp", z, eye).reshape(n_in, n_state), p_in, p_state).astype(BF16)
    expand_c = lambda z: panels(jnp.einsum("gcp,gh->gphc", z.astype(F32), eye).reshape(n_state, n_in),
                                p_state, p_in).astype(BF16)
    return dict(ab_re=ab_re.reshape(1, n_state), ab_im=ab_im.reshape(1, n_state),
                wb_re=expand_b(bb_re), wb_im=expand_b(bb_im),
                wc_re=expand_c(lw["s5_c_re"]), wc_im=expand_c(lw["s5_c_im"]),
                d_skip=lw["s5_d"].astype(F32).reshape(1, n_in),
                w_glu=lw["s5_w_glu"].astype(BF16), b_glu=lw["s5_b_glu"].astype(F32).reshape(1, n_in))


def _s5(u, h_re0, h_im0, sw, tb):
    bsz, t_total, n_in = u.shape
    n_state = sw["ab_re"].shape[1]
    nb = SUBLANES
    assert bsz % nb == 0 and t_total % tb == 0
    n_blocks = t_total // tb
    blocks_per_group = math.gcd(n_state // LANES, 8)
    consts = [sw["ab_re"], sw["ab_im"], sw["wb_re"], sw["wb_im"], sw["wc_re"], sw["wc_im"], sw["d_skip"],
              sw["w_glu"], sw["b_glu"]]
    seq_block = pl.BlockSpec((tb, nb, n_in), lambda g, t: (t, g, 0))
    state_block = pl.BlockSpec((nb, n_state), lambda g, t: (g, 0))
    y, h_re, h_im = pl.pallas_call(
        functools.partial(_s5_kernel, tb=tb, n_blocks=n_blocks, blocks_per_group=blocks_per_group),
        grid=(bsz // nb, n_blocks),
        in_specs=[seq_block, state_block, state_block]
                 + [pl.BlockSpec(cst.shape, lambda g, t, nd=cst.ndim: (0,) * nd) for cst in consts],
        out_specs=[seq_block, state_block, state_block],
        out_shape=[jax.ShapeDtypeStruct((t_total, bsz, n_in), BF16),
                   jax.ShapeDtypeStruct((bsz, n_state), F32),
                   jax.ShapeDtypeStruct((bsz, n_state), F32)],
        scratch_shapes=[pltpu.VMEM((nb * tb, n_state), F32), pltpu.VMEM((nb * tb, n_state), F32),
                        pltpu.VMEM((nb, n_state), F32), pltpu.VMEM((nb, n_state), F32)],
        compiler_params=_params("parallel", "arbitrary"),
        name="s5_scan",
    )(jnp.swapaxes(u, 0, 1), h_re0.astype(F32).reshape(bsz, n_state), h_im0.astype(F32).reshape(bsz, n_state),
      *consts)
    shape = (bsz, n_state // S5_STATE, S5_STATE)
    return jnp.swapaxes(y, 0, 1), h_re.reshape(shape), h_im.reshape(shape)


def _log_sigmoid(z):
    return jnp.minimum(z, 0.0) - jnp.log1p(jnp.exp(-jnp.abs(z)))


def _fox_proj_kernel(x_ref, w_ref, wf_ref, bf_ref, q_ref, k_ref, v_ref, kb_ref, vb_ref, lf_ref, c_ref, carry_sc,
                     *, d, scale, tiles_per_seq, channel_major):
    i = pl.program_id(0)
    tm = x_ref.shape[0]
    xb = x_ref[...].astype(BF16)
    acc = _dot(xb, w_ref[...])
    q_ref[...] = (acc[:, :d] * scale).astype(BF16)
    k = acc[:, d:2 * d]
    v = acc[:, 2 * d:3 * d]
    if channel_major:
        k_ref[...] = k.T
        v_ref[...] = v.T
    else:
        k_ref[...] = k
        v_ref[...] = v
    kb_ref[...] = k.astype(BF16)
    vb_ref[...] = v.astype(BF16)
    lf = _log_sigmoid(_dot(xb, wf_ref[...]) + bf_ref[...])
    lf_ref[...] = lf

    @pl.when(i % tiles_per_seq == 0)
    def _():
        carry_sc[...] = jnp.zeros_like(carry_sc)

    ti = lax.broadcasted_iota(jnp.int32, (tm, tm), 0)
    tj = lax.broadcasted_iota(jnp.int32, (tm, tm), 1)
    c = _dot((ti >= tj).astype(F32), lf, HI) + carry_sc[...]
    c_ref[...] = c
    carry_sc[...] = c[tm - 1:tm, :]


def _fox_proj(x, w_qkv, w_f, b_f, seq_len, tm):
    m, kdim = x.shape
    d = w_qkv.shape[1] // 3
    nh = w_f.shape[1]
    assert seq_len % tm == 0 or tm % seq_len == 0
    tiles_per_seq = max(seq_len // tm, 1)
    channel_major = seq_len % tm == 0 and tm % LANES == 0
    row = lambda n: pl.BlockSpec((tm, n), lambda i: (i, 0))
    if channel_major:
        kv_spec = pl.BlockSpec((None, d, tm), lambda i: (i // tiles_per_seq, 0, i % tiles_per_seq))
        kv_shape = jax.ShapeDtypeStruct((m // seq_len, d, seq_len), F32)
    else:
        kv_spec, kv_shape = row(d), jax.ShapeDtypeStruct((m, d), F32)
    return pl.pallas_call(
        functools.partial(_fox_proj_kernel, d=d, scale=HEAD ** -0.5 * LOG2E, tiles_per_seq=tiles_per_seq,
                          channel_major=channel_major),
        grid=(m // tm,),
        in_specs=[row(kdim), pl.BlockSpec(w_qkv.shape, lambda i: (0, 0)), pl.BlockSpec(w_f.shape, lambda i: (0, 0)),
                  pl.BlockSpec((1, nh), lambda i: (0, 0))],
        out_specs=[row(d), kv_spec, kv_spec, row(d), row(d), row(nh), row(nh)],
        out_shape=[jax.ShapeDtypeStruct((m, d), BF16), kv_shape, kv_shape, jax.ShapeDtypeStruct((m, d), BF16),
                   jax.ShapeDtypeStruct((m, d), BF16), jax.ShapeDtypeStruct((m, nh), F32),
                   jax.ShapeDtypeStruct((m, nh), F32)],
        scratch_shapes=[pltpu.VMEM((1, nh), F32)],
        compiler_params=_params("arbitrary"),
        name="fox_proj",
    )(x, w_qkv, w_f, b_f)


def _fox_attn_kernel(q_ref, k_ref, v_ref, c_ref, o_ref, m_sc, l_sc, acc_sc, *, tq, rq):
    qi = pl.program_id(2)
    n_rb = tq // rq
    heads = LANES // HEAD
    lane = lax.broadcasted_iota(jnp.int32, (1, LANES), 1)
    q = q_ref[...]
    qm = [jnp.where((lane // HEAD) == h, q, jnp.zeros_like(q)) for h in range(heads)]
    m_sc[...] = jnp.full_like(m_sc, -jnp.inf)
    l_sc[...] = jnp.zeros_like(l_sc)
    acc_sc[...] = jnp.zeros_like(acc_sc)

    def chain(h, rb, start, width, masked):
        rows = slice(rb * rq, (rb + 1) * rq)
        kt = k_ref[pl.ds(start, width), :]
        vt = v_ref[pl.ds(start, width), :]
        s = _dot_nt(qm[h][rows], kt) - c_ref[h, :, pl.ds(start, width)]
        if masked:
            ri = lax.broadcasted_iota(jnp.int32, (rq, width), 0) + rb * rq
            ci = lax.broadcasted_iota(jnp.int32, (rq, width), 1)
            s = jnp.where(ci <= ri, s, -jnp.inf)
        m_prev = m_sc[h, rows, :]
        m_new = jnp.maximum(m_prev, jnp.max(s, axis=1, keepdims=True))
        alpha = jnp.exp2(m_prev - m_new)
        p = jnp.exp2(s - jnp.concatenate([m_new] * (width // LANES), axis=1))
        l_sc[h, rows, :] = alpha * l_sc[h, rows, :] + jnp.sum(p, axis=1, keepdims=True)
        acc_sc[h, rows, :] = alpha * acc_sc[h, rows, :] + _dot(p.astype(BF16), vt)
        m_sc[h, rows, :] = m_new

    def body(ki, carry):
        start = pl.multiple_of(ki * tq, tq)
        for h in range(heads):
            for rb in range(n_rb):
                chain(h, rb, start, tq, False)
        return carry

    lax.fori_loop(0, qi, body, 0)
    start = pl.multiple_of(qi * tq, tq)
    for h in range(heads):
        for rb in range(n_rb):
            chain(h, rb, start, (rb + 1) * rq, True)
    out = [acc_sc[h] / l_sc[h] for h in range(heads)]
    o_ref[...] = jnp.where(lane < HEAD, out[0], out[1]).astype(o_ref.dtype)


def _fox_attn(q, kb, vb, c, tq):
    bsz, t_total, d = q.shape
    n_pairs = d // LANES
    hp = LANES // HEAD
    return pl.pallas_call(
        functools.partial(_fox_attn_kernel, tq=tq, rq=tq),
        grid=(bsz, n_pairs, t_total // tq),
        in_specs=[pl.BlockSpec((None, tq, LANES), lambda b, h, i: (b, i, h)),
                  pl.BlockSpec((None, t_total, LANES), lambda b, h, i: (b, 0, h)),
                  pl.BlockSpec((None, t_total, LANES), lambda b, h, i: (b, 0, h)),
                  pl.BlockSpec((None, hp, 1, t_total), lambda b, h, i: (b, h, 0, 0))],
        out_specs=pl.BlockSpec((None, tq, LANES), lambda b, h, i: (b, i, h)),
        out_shape=jax.ShapeDtypeStruct((bsz, t_total, d), BF16),
        scratch_shapes=[pltpu.VMEM((hp, tq, LANES), F32)] * 3,
        compiler_params=_params("parallel", "parallel", "arbitrary"),
        name="fox_attn",
    )(q, kb, vb, c)


def _fox_decode_kernel(pt_ref, q_ref, knew_ref, vnew_ref, cnew_ref, *refs, pages_per_step, n_steps):
    del pt_ref
    npp = pages_per_step
    k_refs, v_refs, lf_refs = refs[:npp], refs[npp:2 * npp], refs[2 * npp:3 * npp]
    o_ref, m_sc, l_sc, acc_sc, carry_sc = refs[3 * npp:]
    j = pl.program_id(1)
    nh, _, page = k_refs[0].shape

    @pl.when(j == 0)
    def _():
        m_sc[...] = jnp.full_like(m_sc, -jnp.inf)
        l_sc[...] = jnp.zeros_like(l_sc)
        acc_sc[...] = jnp.zeros_like(acc_sc)
        carry_sc[...] = cnew_ref[...]

    def scores(k_of_head):
        return jnp.concatenate([jnp.sum(q_ref[h] * k_of_head(h), axis=0, keepdims=True) for h in range(nh)], axis=0)

    later = (lax.broadcasted_iota(jnp.int32, (page, page), 0)
             > lax.broadcasted_iota(jnp.int32, (page, page), 1)).astype(F32)
    for i in range(npp):
        lft = lf_refs[i][...] * LOG2E
        bias = _dot(lft, later, HI) + carry_sc[...]
        carry_sc[...] = carry_sc[...] + jnp.sum(lft, axis=1, keepdims=True)
        s = scores(lambda h, i=i: k_refs[i][h]) + bias
        m_prev = m_sc[...]
        m_new = jnp.maximum(m_prev, jnp.max(s, axis=1, keepdims=True))
        alpha = jnp.exp2(m_prev - m_new)
        p = jnp.exp2(s - m_new)
        l_sc[...] = alpha * l_sc[...] + p
        for h in range(nh):
            acc_sc[h] = alpha[h:h + 1, :] * acc_sc[h] + p[h:h + 1, :] * v_refs[i][h]
        m_sc[...] = m_new

    @pl.when(j == n_steps - 1)
    def _():
        s_new = scores(lambda h: knew_ref[h])
        m_prev = m_sc[...]
        m_new = jnp.maximum(m_prev, s_new)
        alpha = jnp.exp2(m_prev - m_new)
        p_new = jnp.exp2(s_new - m_new)
        l_fin = alpha * jnp.sum(l_sc[...], axis=1, keepdims=True) + p_new
        for h in range(nh):
            num = (alpha[h:h + 1, :1] * jnp.sum(acc_sc[h], axis=1, keepdims=True)
                   + p_new[h:h + 1, :1] * vnew_ref[h])
            o_ref[h] = num / l_fin[h:h + 1, :1]


def _fox_decode(q, k_new, v_new, lf_new, cache_k, cache_v, cache_lf, page_table, layer, pages_per_step):
    bsz, d = q.shape
    nh = d // HEAD
    n_pages = page_table.shape[1]
    page = cache_k.shape[2]
    assert page == LANES
    npp = pages_per_step
    assert n_pages % npp == 0
    n_steps = n_pages // npp
    pt = page_table.reshape(-1).astype(jnp.int32)
    pool_k = jnp.transpose(cache_k, (0, 1, 3, 4, 2))
    pool_v = jnp.transpose(cache_v, (0, 1, 3, 4, 2))
    pool_lf = jnp.transpose(cache_lf, (0, 1, 3, 2))
    column = lambda z: z.astype(F32).reshape(bsz, nh, HEAD, 1)
    q_rep = jnp.broadcast_to(column(q), (bsz, nh, HEAD, LANES))
    c_new = jnp.broadcast_to((lf_new.astype(F32) * LOG2E)[:, :, None], (bsz, nh, LANES))

    def page_map(i, ndim):
        return lambda b, j, pt_ref: (layer, pt_ref[b * n_pages + (n_pages - 1 - (j * npp + i))]) + (0,) * ndim

    per_seq = lambda *blk: pl.BlockSpec((None,) + blk, lambda b, j, pt_ref: (b,) + (0,) * len(blk))
    in_specs = [per_seq(nh, HEAD, LANES), per_seq(nh, HEAD, 1), per_seq(nh, HEAD, 1), per_seq(nh, LANES)]
    in_specs += [pl.BlockSpec((None, None, nh, HEAD, page), page_map(i, 3)) for i in range(npp)]
    in_specs += [pl.BlockSpec((None, None, nh, HEAD, page), page_map(i, 3)) for i in range(npp)]
    in_specs += [pl.BlockSpec((None, None, nh, page), page_map(i, 2)) for i in range(npp)]
    out = pl.pallas_call(
        functools.partial(_fox_decode_kernel, pages_per_step=npp, n_steps=n_steps),
        grid_spec=pltpu.PrefetchScalarGridSpec(
            num_scalar_prefetch=1,
            grid=(bsz, n_steps),
            in_specs=in_specs,
            out_specs=per_seq(nh, HEAD, 1),
            scratch_shapes=[pltpu.VMEM((nh, LANES), F32), pltpu.VMEM((nh, LANES), F32),
                            pltpu.VMEM((nh, HEAD, LANES), F32), pltpu.VMEM((nh, LANES), F32)]),
        out_shape=jax.ShapeDtypeStruct((bsz, nh, HEAD, 1), F32),
        compiler_params=_params("parallel", "arbitrary"),
        name="fox_decode",
    )(pt, q_rep, column(k_new), column(v_new), c_new,
      *([pool_k] * npp), *([pool_v] * npp), *([pool_lf] * npp))
    return out.reshape(bsz, d)


def _mixer_ab(x2d, bsz, seq, shift_prev, wkv0, h_re0, h_im0, lw, tm, s5_tb):
    d_rwkv = lw["mu"].shape[0]
    d_s5 = lw["w_in"].shape[1] - d_rwkv
    p, u = _proj(x2d, lw["w_in"], (d_rwkv, d_s5), (F32, F32), tm)
    y_a, shift, wkv = _rwkv(p.reshape(bsz, seq, d_rwkv), shift_prev, wkv0, lw)
    y_b, h_re, h_im = _s5(u.reshape(bsz, seq, d_s5), h_re0, h_im0, lw["s5"], s5_tb)
    return (y_a.reshape(bsz * seq, -1), y_b.reshape(bsz * seq, -1)), shift, wkv, h_re, h_im


def kernel(x_prompt, x_sample, state_shift, state_wkv, state_s5_re, state_s5_im, cache_k, cache_v, cache_logf, page_table, w_in_ab, mu_shift, w0, w_lora_w, a0, w_lora_a, w_lora_g, k_k, k_a, r_k, lnx_g, lnx_b, s5_lambda_re, s5_lambda_im, s5_log_dt, s5_b_re, s5_b_im, s5_c_re, s5_c_im, s5_d, s5_w_glu, s5_b_glu, w_out_ab, w_in_c, b_forget, w_out_c, ln1_g, ln1_b, ln2_g, ln2_b, ffn_w_gate, ffn_w_up, ffn_w_down):
    bp, seq_p, d_model = x_prompt.shape
    bs, seq_s, _ = x_sample.shape
    assert seq_s == 1
    depth = ln1_g.shape[0]
    alpha = (2.0 * depth) ** 0.25
    n_heads = d_model // HEAD
    mp, ms = bp * seq_p, bs * seq_s
    tm_p, tm_s = _row_tile(mp, 512), _row_tile(ms, 512)
    tm_ffn = _row_tile(mp, 512)
    row = lambda z: z.astype(F32).reshape(1, -1)

    yp = x_prompt.reshape(mp, d_model)
    ys = x_sample.reshape(ms, d_model)

    shift_p, shift_s, wkv_p, wkv_s = [], [], [], []
    s5re_p, s5re_s, s5im_p, s5im_s = [], [], [], []
    k_p, k_s, v_p, v_s, lf_p, lf_s = [], [], [], [], [], []
    for layer in range(depth):
        j = layer // 2
        if layer % 2 == 0:
            lw = dict(w_in=w_in_ab[j].astype(BF16), mu=mu_shift[j], w0=w0[j], w_lora_w=w_lora_w[j], a0=a0[j],
                      w_lora_a=w_lora_a[j], w_lora_g=w_lora_g[j], k_k=k_k[j], k_a=k_a[j], r_k=r_k[j],
                      lnx_g=lnx_g[j], lnx_b=lnx_b[j], s5_lambda_re=s5_lambda_re[j], s5_lambda_im=s5_lambda_im[j],
                      s5_log_dt=s5_log_dt[j], s5_b_re=s5_b_re[j], s5_b_im=s5_b_im[j], s5_c_re=s5_c_re[j],
                      s5_c_im=s5_c_im[j], s5_d=s5_d[j], s5_w_glu=s5_w_glu[j], s5_b_glu=s5_b_glu[j])
            lw["s5"] = _s5_weights(lw)
            w_out = w_out_ab[j].astype(BF16)
            n_s5 = lw["s5"]["ab_re"].shape[1]
            zeros = lambda *shape: jnp.zeros(shape, F32)
            parts_p, sh, wk, hr, hi = _mixer_ab(
                yp, bp, seq_p, zeros(bp, mu_shift.shape[1]), zeros(bp, (w0.shape[1]) // HEAD, HEAD, HEAD),
                zeros(bp, n_s5), zeros(bp, n_s5), lw, tm_p, min(seq_p, 128))
            shift_p.append(sh); wkv_p.append(wk); s5re_p.append(hr); s5im_p.append(hi)
            parts_s, sh, wk, hr, hi = _mixer_ab(
                ys, bs, seq_s, state_shift[j], state_wkv[j], state_s5_re[j], state_s5_im[j], lw, tm_s, 1)
            shift_s.append(sh); wkv_s.append(wk); s5re_s.append(hr); s5im_s.append(hi)
        else:
            w_c = w_in_c[j].astype(BF16)
            w_qkv, w_f = w_c[:, :3 * d_model], w_c[:, 3 * d_model:]
            b_f = row(b_forget[j])
            w_out = w_out_c[j].astype(BF16)
            q, k, v, kb, vb, lf, c = _fox_proj(yp, w_qkv, w_f, b_f, seq_p, tm_p)
            c = jnp.swapaxes(c.reshape(bp, seq_p, n_heads) * LOG2E, 1, 2).reshape(bp, n_heads, 1, seq_p)
            o = _fox_attn(q.reshape(bp, seq_p, d_model), kb.reshape(bp, seq_p, d_model),
                          vb.reshape(bp, seq_p, d_model), c, min(seq_p, 512))
            parts_p = (o.reshape(mp, d_model),)
            if k.shape == (bp, d_model, seq_p):
                heads_last = lambda z: jnp.transpose(z.reshape(bp, n_heads, HEAD, seq_p), (0, 3, 1, 2))
            else:
                heads_last = lambda z: z.reshape(bp, seq_p, n_heads, HEAD)
            k_p.append(heads_last(k)); v_p.append(heads_last(v))
            lf_p.append(lf.reshape(bp, seq_p, n_heads))
            q, k, v, _, _, lf, _ = _fox_proj(ys, w_qkv, w_f, b_f, seq_s, tm_s)
            o = _fox_decode(q.astype(F32), k, v, lf, cache_k, cache_v, cache_logf, page_table, j, 8)
            parts_s = (o,)
            k_s.append(k.reshape(bs, seq_s, n_heads, HEAD)); v_s.append(v.reshape(bs, seq_s, n_heads, HEAD))
            lf_s.append(lf.reshape(bs, seq_s, n_heads))
        g1, b1, g2, b2 = row(ln1_g[layer]), row(ln1_b[layer]), row(ln2_g[layer]), row(ln2_b[layer])
        wg, wu, wd = ffn_w_gate[layer].astype(BF16), ffn_w_up[layer].astype(BF16), ffn_w_down[layer].astype(BF16)
        yp = _outproj_ln(parts_p, w_out, yp, g1, b1, alpha, tm_p)
        ys = _outproj_ln(parts_s, w_out, ys, g1, b1, alpha, tm_s)
        yp = _ffn_ln(yp, wg, wu, wd, g2, b2, alpha, tm_ffn)
        ys = _ffn_ln(ys, wg, wu, wd, g2, b2, alpha, tm_s)
    return (yp.reshape(bp, seq_p, d_model), ys.reshape(bs, seq_s, d_model),
            jnp.stack(shift_p), jnp.stack(shift_s), jnp.stack(wkv_p), jnp.stack(wkv_s),
            jnp.stack(s5re_p), jnp.stack(s5re_s), jnp.stack(s5im_p), jnp.stack(s5im_s),
            jnp.stack(k_p), jnp.stack(k_s), jnp.stack(v_p), jnp.stack(v_s),
            jnp.stack(lf_p), jnp.stack(lf_s))
```

```python
import functools
import math

import jax
import jax.numpy as jnp
from jax import lax
from jax.experimental import pallas as pl
from jax.experimental.pallas import tpu as pltpu

F32, BF16 = jnp.float32, jnp.bfloat16
HI = lax.Precision.HIGHEST

LANES = 128
SUBLANES = 8
VMEM_LIMIT_BYTES = 56 * 1024 * 1024

HEAD = 64
S5_GROUP = 16
S5_STATE = 64
LORA_WA = 128
LORA_G = 128
GN_EPS = 64e-5
LN_EPS = 1e-5
RWKV_CHUNK = 64
LOG2E = math.log2(math.e)

ROW_TILE = 512
S5_TIME_BLOCK = 128
ATTN_Q_TILE = 1024
ATTN_ROW_BLOCK = 512
DECODE_PAGES = 8


def _dot(a, b, precision=None):
    return jnp.dot(a, b, preferred_element_type=F32, precision=precision)


def _dot_nt(a, b, precision=None):
    return lax.dot_general(a, b, (((1,), (1,)), ((), ())), preferred_element_type=F32, precision=precision)


def _dot_tn(a, b, precision=None):
    return lax.dot_general(a, b, (((0,), (0,)), ((), ())), preferred_element_type=F32, precision=precision)


def _split(a, terms=2):
    out = []
    for _ in range(terms - 1):
        piece = a.astype(BF16)
        out.append(piece)
        a = a - piece.astype(F32)
    out.append(a.astype(BF16))
    return tuple(out)


def _mm3(a2, b2, dot=_dot):
    (a_hi, a_lo), (b_hi, b_lo) = a2, b2
    return dot(a_hi, b_hi) + dot(a_hi, b_lo) + dot(a_lo, b_hi)


def _mm_exact_rhs(a_terms, b):
    acc = None
    for piece in a_terms:
        t = _dot(piece, b)
        acc = t if acc is None else acc + t
    return acc


def _params(*semantics):
    return pltpu.CompilerParams(dimension_semantics=semantics, vmem_limit_bytes=VMEM_LIMIT_BYTES)


def _row_tile(m, preferred):
    tm = min(m, preferred)
    assert m % tm == 0, (m, tm)
    return tm


def _layer_norm(z, g, b):
    mu = jnp.mean(z, axis=-1, keepdims=True)
    zc = z - mu
    var = jnp.mean(zc * zc, axis=-1, keepdims=True)
    return zc * lax.rsqrt(var + LN_EPS) * g + b


def _softplus(z):
    return jnp.maximum(z, 0.0) + jnp.log1p(jnp.exp(-jnp.abs(z)))


def _proj_kernel(x_ref, w_ref, *o_refs, widths):
    acc = _dot(x_ref[...].astype(BF16), w_ref[...])
    off = 0
    for o_ref, n in zip(o_refs, widths):
        o_ref[...] = acc[:, off:off + n].astype(o_ref.dtype)
        off += n


def _proj(x, w, widths, dtypes, tm):
    m, k = x.shape
    n = w.shape[1]
    assert sum(widths) == n
    return pl.pallas_call(
        functools.partial(_proj_kernel, widths=tuple(widths)),
        grid=(m // tm,),
        in_specs=[pl.BlockSpec((tm, k), lambda i: (i, 0)), pl.BlockSpec((k, n), lambda i: (0, 0))],
        out_specs=[pl.BlockSpec((tm, wd), lambda i: (i, 0)) for wd in widths],
        out_shape=[jax.ShapeDtypeStruct((m, wd), dt) for wd, dt in zip(widths, dtypes)],
        compiler_params=_params("parallel"),
        name="proj_split",
    )(x, w)


def _outproj_ln_kernel(*refs, n_parts, alpha):
    part_refs = refs[:n_parts]
    w_ref, x_ref, g_ref, b_ref, o_ref = refs[n_parts:]
    off = 0
    mix = None
    for p_ref in part_refs:
        kp = p_ref.shape[1]
        t = _dot(p_ref[...].astype(BF16), w_ref[off:off + kp, :])
        mix = t if mix is None else mix + t
        off += kp
    o_ref[...] = _layer_norm(alpha * x_ref[...] + mix, g_ref[...], b_ref[...])


def _outproj_ln(parts, w, x, g, b, alpha, tm):
    m, d = x.shape
    in_specs = [pl.BlockSpec((tm, p.shape[1]), lambda i: (i, 0)) for p in parts]
    in_specs += [pl.BlockSpec(w.shape, lambda i: (0, 0)), pl.BlockSpec((tm, d), lambda i: (i, 0)),
                 pl.BlockSpec((1, d), lambda i: (0, 0)), pl.BlockSpec((1, d), lambda i: (0, 0))]
    return pl.pallas_call(
        functools.partial(_outproj_ln_kernel, n_parts=len(parts), alpha=alpha),
        grid=(m // tm,),
        in_specs=in_specs,
        out_specs=pl.BlockSpec((tm, d), lambda i: (i, 0)),
        out_shape=jax.ShapeDtypeStruct((m, d), F32),
        compiler_params=_params("parallel"),
        name="outproj_ln",
    )(*parts, w, x, g, b)


def _ffn_chunk(dff):
    for c in (512, 256, 128):
        if dff % c == 0:
            return c
    return dff


def _ffn_ln_kernel(x_ref, wg_ref, wu_ref, wd_ref, g_ref, b_ref, o_ref, *, alpha, chunk):
    x = x_ref[...]
    xb = x.astype(BF16)
    dff = wg_ref.shape[1]
    out = None
    for c in range(dff // chunk):
        cs = slice(c * chunk, (c + 1) * chunk)
        gate = _dot(xb, wg_ref[:, cs])
        up = _dot(xb, wu_ref[:, cs])
        h = (gate * jax.nn.sigmoid(gate) * up).astype(BF16)
        t = _dot(h, wd_ref[cs, :])
        out = t if out is None else out + t
    o_ref[...] = _layer_norm(alpha * x + out, g_ref[...], b_ref[...])


def _ffn_ln(x, wg, wu, wd, g, b, alpha, tm):
    m, d = x.shape
    dff = wg.shape[1]
    const = lambda i: (0, 0)
    return pl.pallas_call(
        functools.partial(_ffn_ln_kernel, alpha=alpha, chunk=_ffn_chunk(dff)),
        grid=(m // tm,),
        in_specs=[pl.BlockSpec((tm, d), lambda i: (i, 0)), pl.BlockSpec((d, dff), const),
                  pl.BlockSpec((d, dff), const), pl.BlockSpec((dff, d), const),
                  pl.BlockSpec((1, d), const), pl.BlockSpec((1, d), const)],
        out_specs=pl.BlockSpec((tm, d), lambda i: (i, 0)),
        out_shape=jax.ShapeDtypeStruct((m, d), F32),
        compiler_params=_params("parallel"),
        name="ffn_ln",
    )(x, wg, wu, wd, g, b)


def _rwkv_kernel(p_ref, sprev_ref, s0_ref, mu_ref, w0_ref, wlw_ref, a0_ref, wla_ref, wlg_ref, kk_ref, ka_ref,
                 rk_ref, gng_ref, gnb_ref, ones_ref, y_ref, shift_ref, sout_ref, carry_sc, state_sc,
                 *, chunk, chunks_per_step, t_total, n_steps):
    c = pl.program_id(1)
    d = y_ref.shape[-1]
    n_pairs = d // LANES
    ell = chunk
    cps = chunks_per_step
    rows = cps * ell

    @pl.when(c == 0)
    def _():
        carry_sc[...] = sprev_ref[...]
        state_sc[...] = s0_ref[...]

    p = p_ref[...]
    row = lax.broadcasted_iota(jnp.int32, (rows, 1), 0)
    prev = jnp.where(row == 0, carry_sc[...], pltpu.roll(p, 1, 0))
    ps = p + (prev - p) * mu_ref[...]
    carry_sc[...] = p[rows - 1:rows, :]

    r = ps[:, 0:d]
    k = ps[:, d:2 * d]
    v = ps[:, 2 * d:3 * d]
    x_wa = ps[:, 3 * d:3 * d + LORA_WA]
    x_g = ps[:, 3 * d + LORA_WA:3 * d + LORA_WA + LORA_G]
    ones_blk = ones_ref[...]
    head_sum = lambda z: _mm_exact_rhs(_split(z), ones_blk)

    w_log = -_softplus(-(w0_ref[...] + _mm3(_split(jnp.tanh(x_wa)), (wlw_ref[0], wlw_ref[1])))) - 0.5
    logw = -jnp.exp(w_log)
    a = jax.nn.sigmoid(a0_ref[...] + _mm3(_split(x_wa), (wla_ref[0], wla_ref[1])))
    g = _mm3(_split(jax.nn.sigmoid(x_g)), (wlg_ref[0], wlg_ref[1]))
    kk = k * kk_ref[...]
    kk = kk / jnp.maximum(jnp.sqrt(head_sum(kk * kk)), 1e-12)
    k2 = k * (1.0 + (a - 1.0) * ka_ref[...])
    bv = kk * a
    if t_total % rows:
        valid = (c * rows + row) < t_total
        logw = jnp.where(valid, logw, 0.0)
        kk = jnp.where(valid, kk, 0.0)
        bv = jnp.where(valid, bv, 0.0)
        k2 = jnp.where(valid, k2, 0.0)

    ti = lax.broadcasted_iota(jnp.int32, (rows, rows), 0)
    tj = lax.broadcasted_iota(jnp.int32, (rows, rows), 1)
    tri = ((ti >= tj) & (ti // ell == tj // ell)).astype(BF16)
    cl = None
    for piece in _split(logw, 3):
        t = _dot(tri, piece)
        cl = t if cl is None else cl + t
    e_in = jnp.exp(cl)
    e_neg = jnp.exp(-cl)
    a_t = _split(-kk * jnp.exp(cl - logw))
    r_t = _split(r * e_in)
    b_t = _split(bv * e_neg)
    k_t = _split(k2 * e_neg)
    v_t = _split(v)

    lane = lax.broadcasted_iota(jnp.int32, (1, LANES), 1)
    first = lane < HEAD
    si = lax.broadcasted_iota(jnp.int32, (2 * ell, 2 * ell), 0) % ell
    sj = lax.broadcasted_iota(jnp.int32, (2 * ell, 2 * ell), 1) % ell
    strict = si > sj
    incl = si >= sj
    eye = (lax.broadcasted_iota(jnp.int32, (2 * ell, 2 * ell), 0)
           == lax.broadcasted_iota(jnp.int32, (2 * ell, 2 * ell), 1)).astype(F32)

    def stack(z2, ch, sl):
        zero = jnp.zeros((ell, LANES), BF16)
        tok = slice(ch * ell, (ch + 1) * ell)
        return tuple(jnp.concatenate([jnp.where(first, z[tok, sl], zero), jnp.where(first, zero, z[tok, sl])], axis=0)
                     for z in z2)

    cat = lambda x2, y2: tuple(jnp.concatenate([x, y], axis=0) for x, y in zip(x2, y2))

    pairs = range(n_pairs)
    sls = [slice(pr * LANES, (pr + 1) * LANES) for pr in pairs]
    units = [(ch, pr) for ch in range(cps) for pr in pairs]
    a_s = {u_: stack(a_t, u_[0], sls[u_[1]]) for u_ in units}
    r_s = {u_: stack(r_t, u_[0], sls[u_[1]]) for u_ in units}
    v_s = {u_: stack(v_t, u_[0], sls[u_[1]]) for u_ in units}
    bk_s = {u_: cat(stack(b_t, u_[0], sls[u_[1]]), stack(k_t, u_[0], sls[u_[1]])) for u_ in units}
    gram = {u_: _mm3(cat(a_s[u_], r_s[u_]), bk_s[u_], _dot_nt) for u_ in units}
    n_ab = {u_: jnp.where(strict, gram[u_][:2 * ell, :2 * ell], 0.0) for u_ in units}
    n_ak = {u_: jnp.where(strict, gram[u_][:2 * ell, 2 * ell:], 0.0) for u_ in units}
    n_rb = {u_: jnp.where(incl, gram[u_][2 * ell:, :2 * ell], 0.0) for u_ in units}
    n_rk = {u_: jnp.where(incl, gram[u_][2 * ell:, 2 * ell:], 0.0) for u_ in units}
    inv = {u_: eye + n_ab[u_] for u_ in units}
    pw = dict(n_ab)
    for _ in range(int(math.log2(ell)) - 1):
        pw2 = {u_: _split(pw[u_]) for u_ in units}
        pw = {u_: _mm3(pw2[u_], pw2[u_]) for u_ in units}
        inv = {u_: inv[u_] + _mm3(_split(pw[u_]), _split(inv[u_])) for u_ in units}
    state = [state_sc[pr] for pr in pairs]
    y_chunks = []
    for ch in range(cps):
        tok = slice(ch * ell, (ch + 1) * ell)
        w_end = e_in[(ch + 1) * ell - 1:(ch + 1) * ell, :]
        state2 = [_split(s) for s in state]
        x = [_mm3(a_s[ch, pr], state2[pr], _dot_nt) + _mm3(_split(n_ak[ch, pr]), v_s[ch, pr]) for pr in pairs]
        u = [_mm3(_split(inv[ch, pr]), _split(x[pr])) for pr in pairs]
        y2 = [_mm3(r_s[ch, pr], state2[pr], _dot_nt) + _mm3(_split(n_rb[ch, pr]), _split(u[pr]))
              + _mm3(_split(n_rk[ch, pr]), v_s[ch, pr]) for pr in pairs]
        new_state = []
        for pr in pairs:
            sl = sls[pr]
            v_f32 = jnp.concatenate([jnp.where(first, v[tok, sl], 0.0), jnp.where(first, 0.0, v[tok, sl])], axis=0)
            uv_t = jnp.concatenate([u[pr], v_f32], axis=0).T
            new_state.append((state[pr] + _mm3(_split(uv_t), bk_s[ch, pr])) * w_end[:, sl])
        state = new_state
        y_chunks.append(jnp.concatenate([m[:ell] + m[ell:] for m in y2], axis=1))
    for pr in pairs:
        state_sc[pr] = state[pr]

    y = jnp.concatenate(y_chunks, axis=0)
    inv_head = 1.0 / HEAD
    mean = head_sum(y) * inv_head
    yc = y - mean
    var = head_sum(yc * yc) * inv_head
    yn = yc * lax.rsqrt(var + GN_EPS) * gng_ref[...] + gnb_ref[...]
    bonus = head_sum(r * k2 * rk_ref[...]) * v
    y_ref[...] = ((yn + bonus) * g).astype(y_ref.dtype)

    @pl.when(c == n_steps - 1)
    def _():
        last = (t_total - 1) % rows
        shift_ref[...] = p[last:last + 1, :]
        sout_ref[...] = state_sc[...]


def _pad_rows(w, rows, offset):
    out = jnp.zeros((rows, w.shape[1]), F32)
    return out.at[offset:offset + w.shape[0]].set(w.astype(F32))


def _rwkv(p, shift_prev, wkv0, lw):
    bsz, t_total, width = p.shape
    d = width - LORA_WA - LORA_G
    d //= 3
    n_pairs = d // LANES
    ell = RWKV_CHUNK
    n_chunks = -(-t_total // ell)
    cps = next(n for n in (4, 2, 1) if n_chunks % n == 0)
    rows = cps * ell
    n_steps = n_chunks // cps
    t_pad = n_chunks * ell
    if t_pad != t_total:
        p = jnp.pad(p, ((0, 0), (0, t_pad - t_total), (0, 0)))
    s0 = wkv0.astype(F32).reshape(bsz, n_pairs, 2, HEAD, HEAD)
    zero = jnp.zeros_like(s0[:, :, 0])
    s0 = jnp.concatenate([jnp.concatenate([s0[:, :, 0], zero], axis=-1),
                          jnp.concatenate([zero, s0[:, :, 1]], axis=-1)], axis=-2)
    row = lambda z: z.astype(F32).reshape(1, -1)
    ones_blk = jnp.kron(jnp.eye(d // HEAD, dtype=F32), jnp.ones((HEAD, HEAD), F32)).astype(BF16)
    hi_lo = lambda w: jnp.stack(_split(w))
    consts = [row(lw["mu"]), row(lw["w0"]), hi_lo(_pad_rows(lw["w_lora_w"], LORA_WA, 0)), row(lw["a0"]),
              hi_lo(_pad_rows(lw["w_lora_a"], LORA_WA, LORA_WA // 2)), hi_lo(lw["w_lora_g"].astype(F32)),
              row(lw["k_k"]), row(lw["k_a"]), row(lw["r_k"]), row(lw["lnx_g"]), row(lw["lnx_b"]), ones_blk]
    const_specs = [pl.BlockSpec(cst.shape, lambda b, c, nd=cst.ndim: (0,) * nd) for cst in consts]
    y, shift, sout = pl.pallas_call(
        functools.partial(_rwkv_kernel, chunk=ell, chunks_per_step=cps, t_total=t_total, n_steps=n_steps),
        grid=(bsz, n_steps),
        in_specs=[pl.BlockSpec((None, rows, width), lambda b, c: (b, c, 0)),
                  pl.BlockSpec((None, 1, width), lambda b, c: (b, 0, 0)),
                  pl.BlockSpec((None, n_pairs, LANES, LANES), lambda b, c: (b, 0, 0, 0))] + const_specs,
        out_specs=[pl.BlockSpec((None, rows, d), lambda b, c: (b, c, 0)),
                   pl.BlockSpec((None, 1, width), lambda b, c: (b, 0, 0)),
                   pl.BlockSpec((None, n_pairs, LANES, LANES), lambda b, c: (b, 0, 0, 0))],
        out_shape=[jax.ShapeDtypeStruct((bsz, t_pad, d), BF16),
                   jax.ShapeDtypeStruct((bsz, 1, width), F32),
                   jax.ShapeDtypeStruct((bsz, n_pairs, LANES, LANES), F32)],
        scratch_shapes=[pltpu.VMEM((1, width), F32), pltpu.VMEM((n_pairs, LANES, LANES), F32)],
        compiler_params=_params("parallel", "arbitrary"),
        name="rwkv7_chunked",
    )(p, shift_prev.astype(F32).reshape(bsz, 1, width), s0, *consts)
    wkv = jnp.stack([sout[:, :, :HEAD, :HEAD], sout[:, :, HEAD:, HEAD:]], axis=2)
    return y[:, :t_total], shift[:, 0], wkv.reshape(bsz, 2 * n_pairs, HEAD, HEAD)


def _gelu_tanh(x):
    return 0.5 * x * (1.0 + jnp.tanh(math.sqrt(2.0 / math.pi) * (x + 0.044715 * (x * x * x))))


def _s5_kernel(u_ref, h0re_ref, h0im_ref, abre_ref, abim_ref, wbre_ref, wbim_ref, wcre_ref, wcim_ref, dskip_ref,
               wglu_ref, bglu_ref, y_ref, hre_ref, him_ref, ere_sc, eim_sc, hre_sc, him_sc,
               *, tb, n_blocks, blocks_per_group):
    tblk = pl.program_id(1)
    nb = SUBLANES
    n_state = hre_sc.shape[1]
    n_in = u_ref.shape[-1]
    n_panels = wbre_ref.shape[0]
    p_in, p_state = n_in // n_panels, n_state // n_panels

    @pl.when(tblk == 0)
    def _():
        hre_sc[...] = h0re_ref[...]
        him_sc[...] = h0im_ref[...]

    u = u_ref[...].reshape(tb * nb, n_in)
    ub = u.astype(BF16)
    for pn in range(n_panels):
        up = ub[:, pn * p_in:(pn + 1) * p_in]
        ere_sc[:, pn * p_state:(pn + 1) * p_state] = _dot(up, wbre_ref[pn])
        eim_sc[:, pn * p_state:(pn + 1) * p_state] = _dot(up, wbim_ref[pn])

    n_blk = n_state // LANES
    for cg in range(n_blk // blocks_per_group):
        blks = range(cg * blocks_per_group, (cg + 1) * blocks_per_group)
        a_re = [jnp.broadcast_to(abre_ref[:, jb * LANES:(jb + 1) * LANES], (nb, LANES)) for jb in blks]
        a_im = [jnp.broadcast_to(abim_ref[:, jb * LANES:(jb + 1) * LANES], (nb, LANES)) for jb in blks]

        def step(t, carry, blks=blks, a_re=a_re, a_im=a_im):
            rows = pl.ds(pl.multiple_of(t * nb, nb), nb)
            new_re, new_im = [], []
            for n, jb in enumerate(blks):
                cs = slice(jb * LANES, (jb + 1) * LANES)
                h_re, h_im = carry[0][n], carry[1][n]
                n_re = a_re[n] * h_re - a_im[n] * h_im + ere_sc[rows, cs]
                n_im = a_re[n] * h_im + a_im[n] * h_re + eim_sc[rows, cs]
                ere_sc[rows, cs] = n_re
                eim_sc[rows, cs] = n_im
                new_re.append(n_re)
                new_im.append(n_im)
            return tuple(new_re), tuple(new_im)

        init = (tuple(hre_sc[:, jb * LANES:(jb + 1) * LANES] for jb in blks),
                tuple(him_sc[:, jb * LANES:(jb + 1) * LANES] for jb in blks))
        fin_re, fin_im = lax.fori_loop(0, tb, step, init, unroll=2 if tb % 2 == 0 else 1)
        for n, jb in enumerate(blks):
            hre_sc[:, jb * LANES:(jb + 1) * LANES] = fin_re[n]
            him_sc[:, jb * LANES:(jb + 1) * LANES] = fin_im[n]

    y_parts = []
    for pn in range(n_panels):
        ss = slice(pn * p_state, (pn + 1) * p_state)
        y_parts.append(_dot(ere_sc[:, ss].astype(BF16), wcre_ref[pn]) - _dot(eim_sc[:, ss].astype(BF16), wcim_ref[pn]))
    y = jnp.concatenate(y_parts, axis=1) + dskip_ref[...] * u
    y = _gelu_tanh(y)
    y = y * jax.nn.sigmoid(_dot(y.astype(BF16), wglu_ref[...]) + bglu_ref[...])
    y_ref[...] = y.reshape(tb, nb, n_in).astype(y_ref.dtype)

    @pl.when(tblk == n_blocks - 1)
    def _():
        hre_ref[...] = hre_sc[...]
        him_ref[...] = him_sc[...]


def _s5_weights(lw):
    lr, li = lw["s5_lambda_re"].astype(F32), lw["s5_lambda_im"].astype(F32)
    n_groups = lr.shape[0]
    dt = jnp.exp(lw["s5_log_dt"].astype(F32))[:, None]
    mag = jnp.exp(lr * dt)
    ab_re, ab_im = mag * jnp.cos(li * dt), mag * jnp.sin(li * dt)
    den = lr * lr + li * li
    n_re = ab_re - 1.0
    f_re = (n_re * lr + ab_im * li) / den
    f_im = (ab_im * lr - n_re * li) / den
    br, bi = lw["s5_b_re"].astype(F32), lw["s5_b_im"].astype(F32)
    bb_re = f_re[..., None] * br - f_im[..., None] * bi
    bb_im = f_re[..., None] * bi + f_im[..., None] * br
    eye = jnp.eye(n_groups, dtype=F32)
    n_in, n_state = n_groups * S5_GROUP, n_groups * S5_STATE
    n_panels = max(1, n_in // (2 * LANES))
    p_in, p_state = n_in // n_panels, n_state // n_panels

    def panels(w, rows, cols):
        return jnp.stack([w[pn * rows:(pn + 1) * rows, pn * cols:(pn + 1) * cols] for pn in range(n_panels)])

    expand_b = lambda z: panels(jnp.einsum("gpc,gh->gchp", z, eye).reshape(n_in, n_state), p_in, p_state).astype(BF16)
    expand_c = lambda z: panels(jnp.einsum("gcp,gh->gphc", z.astype(F32), eye).reshape(n_state, n_in),
                                p_state, p_in).astype(BF16)
    return dict(ab_re=ab_re.reshape(1, n_state), ab_im=ab_im.reshape(1, n_state),
                wb_re=expand_b(bb_re), wb_im=expand_b(bb_im),
                wc_re=expand_c(lw["s5_c_re"]), wc_im=expand_c(lw["s5_c_im"]),
                d_skip=lw["s5_d"].astype(F32).reshape(1, n_in),
                w_glu=lw["s5_w_glu"].astype(BF16), b_glu=lw["s5_b_glu"].astype(F32).reshape(1, n_in))


def _s5(u, h_re0, h_im0, sw, tb):
    bsz, t_total, n_in = u.shape
    n_state = sw["ab_re"].shape[1]
    nb = SUBLANES
    assert bsz % nb == 0 and t_total % tb == 0
    n_blocks = t_total // tb
    blocks_per_group = math.gcd(n_state // LANES, 8)
    consts = [sw["ab_re"], sw["ab_im"], sw["wb_re"], sw["wb_im"], sw["wc_re"], sw["wc_im"], sw["d_skip"],
              sw["w_glu"], sw["b_glu"]]
    seq_block = pl.BlockSpec((tb, nb, n_in), lambda g, t: (t, g, 0))
    state_block = pl.BlockSpec((nb, n_state), lambda g, t: (g, 0))
    y, h_re, h_im = pl.pallas_call(
        functools.partial(_s5_kernel, tb=tb, n_blocks=n_blocks, blocks_per_group=blocks_per_group),
        grid=(bsz // nb, n_blocks),
        in_specs=[seq_block, state_block, state_block]
                 + [pl.BlockSpec(cst.shape, lambda g, t, nd=cst.ndim: (0,) * nd) for cst in consts],
        out_specs=[seq_block, state_block, state_block],
        out_shape=[jax.ShapeDtypeStruct((t_total, bsz, n_in), BF16),
                   jax.ShapeDtypeStruct((bsz, n_state), F32),
                   jax.ShapeDtypeStruct((bsz, n_state), F32)],
        scratch_shapes=[pltpu.VMEM((nb * tb, n_state), F32), pltpu.VMEM((nb * tb, n_state), F32),
                        pltpu.VMEM((nb, n_state), F32), pltpu.VMEM((nb, n_state), F32)],
        compiler_params=_params("parallel", "arbitrary"),
        name="s5_scan",
    )(jnp.swapaxes(u, 0, 1), h_re0.astype(F32).reshape(bsz, n_state), h_im0.astype(F32).reshape(bsz, n_state),
      *consts)
    shape = (bsz, n_state // S5_STATE, S5_STATE)
    return jnp.swapaxes(y, 0, 1), h_re.reshape(shape), h_im.reshape(shape)


def _log_sigmoid(z):
    return jnp.minimum(z, 0.0) - jnp.log1p(jnp.exp(-jnp.abs(z)))


def _fox_proj_kernel(x_ref, w_ref, wf_ref, bf_ref, q_ref, k_ref, v_ref, kb_ref, vb_ref, lf_ref, c_ref, carry_sc,
                     *, d, scale, tiles_per_seq, channel_major):
    i = pl.program_id(0)
    tm = x_ref.shape[0]
    xb = x_ref[...].astype(BF16)
    acc = _dot(xb, w_ref[...])
    q_ref[...] = (acc[:, :d] * scale).astype(BF16)
    k = acc[:, d:2 * d]
    v = acc[:, 2 * d:3 * d]
    if channel_major:
        k_ref[...] = k.T
        v_ref[...] = v.T
    else:
        k_ref[...] = k
        v_ref[...] = v
    kb_ref[...] = k.astype(BF16)
    vb_ref[...] = v.astype(BF16)
    lf = _log_sigmoid(_dot(xb, wf_ref[...]) + bf_ref[...])
    lf_ref[...] = lf

    @pl.when(i % tiles_per_seq == 0)
    def _():
        carry_sc[...] = jnp.zeros_like(carry_sc)

    ti = lax.broadcasted_iota(jnp.int32, (tm, tm), 0)
    tj = lax.broadcasted_iota(jnp.int32, (tm, tm), 1)
    tri = (ti >= tj).astype(BF16)
    c = carry_sc[...]
    for piece in _split(lf, 3):
        c = c + _dot(tri, piece)
    c_ref[...] = c
    carry_sc[...] = c[tm - 1:tm, :]


def _fox_proj(x, w_qkv, w_f, b_f, seq_len, tm):
    m, kdim = x.shape
    d = w_qkv.shape[1] // 3
    nh = w_f.shape[1]
    assert seq_len % tm == 0 or tm % seq_len == 0
    tiles_per_seq = max(seq_len // tm, 1)
    channel_major = seq_len % tm == 0 and tm % LANES == 0
    row = lambda n: pl.BlockSpec((tm, n), lambda i: (i, 0))
    if channel_major:
        kv_spec = pl.BlockSpec((None, d, tm), lambda i: (i // tiles_per_seq, 0, i % tiles_per_seq))
        kv_shape = jax.ShapeDtypeStruct((m // seq_len, d, seq_len), F32)
    else:
        kv_spec, kv_shape = row(d), jax.ShapeDtypeStruct((m, d), F32)
    return pl.pallas_call(
        functools.partial(_fox_proj_kernel, d=d, scale=HEAD ** -0.5 * LOG2E, tiles_per_seq=tiles_per_seq,
                          channel_major=channel_major),
        grid=(m // tm,),
        in_specs=[row(kdim), pl.BlockSpec(w_qkv.shape, lambda i: (0, 0)), pl.BlockSpec(w_f.shape, lambda i: (0, 0)),
                  pl.BlockSpec((1, nh), lambda i: (0, 0))],
        out_specs=[row(d), kv_spec, kv_spec, row(d), row(d), row(nh), row(nh)],
        out_shape=[jax.ShapeDtypeStruct((m, d), BF16), kv_shape, kv_shape, jax.ShapeDtypeStruct((m, d), BF16),
                   jax.ShapeDtypeStruct((m, d), BF16), jax.ShapeDtypeStruct((m, nh), F32),
                   jax.ShapeDtypeStruct((m, nh), F32)],
        scratch_shapes=[pltpu.VMEM((1, nh), F32)],
        compiler_params=_params("arbitrary"),
        name="fox_proj",
    )(x, w_qkv, w_f, b_f)


def _fox_attn_kernel(q_ref, k_ref, v_ref, c_ref, o_ref, m_sc, l_sc, acc_sc, *, tq, rq):
    qi = pl.program_id(2)
    n_rb = tq // rq
    heads = LANES // HEAD
    lane = lax.broadcasted_iota(jnp.int32, (1, LANES), 1)
    q = q_ref[...]
    qm = [jnp.where((lane // HEAD) == h, q, jnp.zeros_like(q)) for h in range(heads)]
    m_sc[...] = jnp.full_like(m_sc, -jnp.inf)
    l_sc[...] = jnp.zeros_like(l_sc)
    acc_sc[...] = jnp.zeros_like(acc_sc)

    def chain(h, rb, start, width, masked):
        rows = slice(rb * rq, (rb + 1) * rq)
        kt = k_ref[pl.ds(start, width), :]
        vt = v_ref[pl.ds(start, width), :]
        s = _dot_nt(qm[h][rows], kt) - c_ref[h, :, pl.ds(start, width)]
        if masked:
            ri = lax.broadcasted_iota(jnp.int32, (rq, width), 0) + rb * rq
            ci = lax.broadcasted_iota(jnp.int32, (rq, width), 1)
            s = jnp.where(ci <= ri, s, -jnp.inf)
        m_prev = m_sc[h, rows, :]
        m_new = jnp.maximum(m_prev, jnp.max(s, axis=1, keepdims=True))
        alpha = jnp.exp2(m_prev - m_new)
        p = jnp.exp2(s - jnp.concatenate([m_new] * (width // LANES), axis=1))
        l_sc[h, rows, :] = alpha * l_sc[h, rows, :] + jnp.sum(p, axis=1, keepdims=True)
        acc_sc[h, rows, :] = alpha * acc_sc[h, rows, :] + _dot(p.astype(BF16), vt)
        m_sc[h, rows, :] = m_new

    def body(ki, carry):
        start = pl.multiple_of(ki * tq, tq)
        for h in range(heads):
            for rb in range(n_rb):
                chain(h, rb, start, tq, False)
        return carry

    lax.fori_loop(0, qi, body, 0)
    start = pl.multiple_of(qi * tq, tq)
    for h in range(heads):
        for rb in range(n_rb):
            chain(h, rb, start, (rb + 1) * rq, True)
    out = [acc_sc[h] / l_sc[h] for h in range(heads)]
    o_ref[...] = jnp.where(lane < HEAD, out[0], out[1]).astype(o_ref.dtype)


def _fox_attn(q, kb, vb, c, tq):
    bsz, t_total, d = q.shape
    n_pairs = d // LANES
    hp = LANES // HEAD
    return pl.pallas_call(
        functools.partial(_fox_attn_kernel, tq=tq, rq=min(tq, ATTN_ROW_BLOCK)),
        grid=(bsz, n_pairs, t_total // tq),
        in_specs=[pl.BlockSpec((None, tq, LANES), lambda b, h, i: (b, i, h)),
                  pl.BlockSpec((None, t_total, LANES), lambda b, h, i: (b, 0, h)),
                  pl.BlockSpec((None, t_total, LANES), lambda b, h, i: (b, 0, h)),
                  pl.BlockSpec((None, hp, 1, t_total), lambda b, h, i: (b, h, 0, 0))],
        out_specs=pl.BlockSpec((None, tq, LANES), lambda b, h, i: (b, i, h)),
        out_shape=jax.ShapeDtypeStruct((bsz, t_total, d), BF16),
        scratch_shapes=[pltpu.VMEM((hp, tq, LANES), F32)] * 3,
        compiler_params=_params("parallel", "parallel", "arbitrary"),
        name="fox_attn",
    )(q, kb, vb, c)


def _fox_decode_kernel(pt_ref, q_ref, knew_ref, vnew_ref, cnew_ref, *refs, pages_per_step, n_steps):
    del pt_ref
    npp = pages_per_step
    k_refs, v_refs, lf_refs = refs[:npp], refs[npp:2 * npp], refs[2 * npp:3 * npp]
    o_ref, m_sc, l_sc, acc_sc, carry_sc = refs[3 * npp:]
    j = pl.program_id(1)
    nh, _, page = k_refs[0].shape

    @pl.when(j == 0)
    def _():
        m_sc[...] = jnp.full_like(m_sc, -jnp.inf)
        l_sc[...] = jnp.zeros_like(l_sc)
        acc_sc[...] = jnp.zeros_like(acc_sc)
        carry_sc[...] = cnew_ref[...]

    def scores(k_of_head):
        return jnp.concatenate([jnp.sum(q_ref[h] * k_of_head(h), axis=0, keepdims=True) for h in range(nh)], axis=0)

    later = (lax.broadcasted_iota(jnp.int32, (page, page), 0)
             > lax.broadcasted_iota(jnp.int32, (page, page), 1)).astype(F32)
    for i in range(npp):
        lft = lf_refs[i][...] * LOG2E
        bias = _dot(lft, later, HI) + carry_sc[...]
        carry_sc[...] = carry_sc[...] + jnp.sum(lft, axis=1, keepdims=True)
        s = scores(lambda h, i=i: k_refs[i][h]) + bias
        m_prev = m_sc[...]
        m_new = jnp.maximum(m_prev, jnp.max(s, axis=1, keepdims=True))
        alpha = jnp.exp2(m_prev - m_new)
        p = jnp.exp2(s - m_new)
        l_sc[...] = alpha * l_sc[...] + p
        for h in range(nh):
            acc_sc[h] = alpha[h:h + 1, :] * acc_sc[h] + p[h:h + 1, :] * v_refs[i][h]
        m_sc[...] = m_new

    @pl.when(j == n_steps - 1)
    def _():
        s_new = scores(lambda h: knew_ref[h])
        m_prev = m_sc[...]
        m_new = jnp.maximum(m_prev, s_new)
        alpha = jnp.exp2(m_prev - m_new)
        p_new = jnp.exp2(s_new - m_new)
        l_fin = alpha * jnp.sum(l_sc[...], axis=1, keepdims=True) + p_new
        for h in range(nh):
            num = (alpha[h:h + 1, :1] * jnp.sum(acc_sc[h], axis=1, keepdims=True)
                   + p_new[h:h + 1, :1] * vnew_ref[h])
            o_ref[h] = num / l_fin[h:h + 1, :1]


def _fox_decode(q, k_new, v_new, lf_new, cache_k, cache_v, cache_lf, page_table, layer, pages_per_step):
    bsz, d = q.shape
    nh = d // HEAD
    n_pages = page_table.shape[1]
    page = cache_k.shape[2]
    assert page == LANES
    npp = pages_per_step
    assert n_pages % npp == 0
    n_steps = n_pages // npp
    pt = page_table.reshape(-1).astype(jnp.int32)
    pool_k = jnp.transpose(cache_k, (0, 1, 3, 4, 2))
    pool_v = jnp.transpose(cache_v, (0, 1, 3, 4, 2))
    pool_lf = jnp.transpose(cache_lf, (0, 1, 3, 2))
    column = lambda z: z.astype(F32).reshape(bsz, nh, HEAD, 1)
    q_rep = jnp.broadcast_to(column(q), (bsz, nh, HEAD, LANES))
    c_new = jnp.broadcast_to((lf_new.astype(F32) * LOG2E)[:, :, None], (bsz, nh, LANES))

    def page_map(i, ndim):
        return lambda b, j, pt_ref: (layer, pt_ref[b * n_pages + (n_pages - 1 - (j * npp + i))]) + (0,) * ndim

    per_seq = lambda *blk: pl.BlockSpec((None,) + blk, lambda b, j, pt_ref: (b,) + (0,) * len(blk))
    in_specs = [per_seq(nh, HEAD, LANES), per_seq(nh, HEAD, 1), per_seq(nh, HEAD, 1), per_seq(nh, LANES)]
    in_specs += [pl.BlockSpec((None, None, nh, HEAD, page), page_map(i, 3)) for i in range(npp)]
    in_specs += [pl.BlockSpec((None, None, nh, HEAD, page), page_map(i, 3)) for i in range(npp)]
    in_specs += [pl.BlockSpec((None, None, nh, page), page_map(i, 2)) for i in range(npp)]
    out = pl.pallas_call(
        functools.partial(_fox_decode_kernel, pages_per_step=npp, n_steps=n_steps),
        grid_spec=pltpu.PrefetchScalarGridSpec(
            num_scalar_prefetch=1,
            grid=(bsz, n_steps),
            in_specs=in_specs,
            out_specs=per_seq(nh, HEAD, 1),
            scratch_shapes=[pltpu.VMEM((nh, LANES), F32), pltpu.VMEM((nh, LANES), F32),
                            pltpu.VMEM((nh, HEAD, LANES), F32), pltpu.VMEM((nh, LANES), F32)]),
        out_shape=jax.ShapeDtypeStruct((bsz, nh, HEAD, 1), F32),
        compiler_params=_params("parallel", "arbitrary"),
        name="fox_decode",
    )(pt, q_rep, column(k_new), column(v_new), c_new,
      *([pool_k] * npp), *([pool_v] * npp), *([pool_lf] * npp))
    return out.reshape(bsz, d)


def _mixer_ab(x2d, bsz, seq, shift_prev, wkv0, h_re0, h_im0, lw, tm, s5_tb):
    d_rwkv = lw["mu"].shape[0]
    d_s5 = lw["w_in"].shape[1] - d_rwkv
    p, u = _proj(x2d, lw["w_in"], (d_rwkv, d_s5), (F32, F32), tm)
    y_a, shift, wkv = _rwkv(p.reshape(bsz, seq, d_rwkv), shift_prev, wkv0, lw)
    y_b, h_re, h_im = _s5(u.reshape(bsz, seq, d_s5), h_re0, h_im0, lw["s5"], s5_tb)
    return (y_a.reshape(bsz * seq, -1), y_b.reshape(bsz * seq, -1)), shift, wkv, h_re, h_im


def kernel(x_prompt, x_sample, state_shift, state_wkv, state_s5_re, state_s5_im, cache_k, cache_v, cache_logf, page_table, w_in_ab, mu_shift, w0, w_lora_w, a0, w_lora_a, w_lora_g, k_k, k_a, r_k, lnx_g, lnx_b, s5_lambda_re, s5_lambda_im, s5_log_dt, s5_b_re, s5_b_im, s5_c_re, s5_c_im, s5_d, s5_w_glu, s5_b_glu, w_out_ab, w_in_c, b_forget, w_out_c, ln1_g, ln1_b, ln2_g, ln2_b, ffn_w_gate, ffn_w_up, ffn_w_down):
    bp, seq_p, d_model = x_prompt.shape
    bs, seq_s, _ = x_sample.shape
    assert seq_s == 1
    depth = ln1_g.shape[0]
    alpha = (2.0 * depth) ** 0.25
    n_heads = d_model // HEAD
    mp, ms = bp * seq_p, bs * seq_s
    tm_p, tm_s = _row_tile(mp, ROW_TILE), _row_tile(ms, ROW_TILE)
    row = lambda z: z.astype(F32).reshape(1, -1)

    yp = x_prompt.reshape(mp, d_model)
    ys = x_sample.reshape(ms, d_model)

    shift_p, shift_s, wkv_p, wkv_s = [], [], [], []
    s5re_p, s5re_s, s5im_p, s5im_s = [], [], [], []
    k_p, k_s, v_p, v_s, lf_p, lf_s = [], [], [], [], [], []
    for layer in range(depth):
        j = layer // 2
        if layer % 2 == 0:
            lw = dict(w_in=w_in_ab[j].astype(BF16), mu=mu_shift[j], w0=w0[j], w_lora_w=w_lora_w[j], a0=a0[j],
                      w_lora_a=w_lora_a[j], w_lora_g=w_lora_g[j], k_k=k_k[j], k_a=k_a[j], r_k=r_k[j],
                      lnx_g=lnx_g[j], lnx_b=lnx_b[j], s5_lambda_re=s5_lambda_re[j], s5_lambda_im=s5_lambda_im[j],
                      s5_log_dt=s5_log_dt[j], s5_b_re=s5_b_re[j], s5_b_im=s5_b_im[j], s5_c_re=s5_c_re[j],
                      s5_c_im=s5_c_im[j], s5_d=s5_d[j], s5_w_glu=s5_w_glu[j], s5_b_glu=s5_b_glu[j])
            lw["s5"] = _s5_weights(lw)
            w_out = w_out_ab[j].astype(BF16)
            n_s5 = lw["s5"]["ab_re"].shape[1]
            zeros = lambda *shape: jnp.zeros(shape, F32)
            parts_p, sh, wk, hr, hi = _mixer_ab(
                yp, bp, seq_p, zeros(bp, mu_shift.shape[1]), zeros(bp, (w0.shape[1]) // HEAD, HEAD, HEAD),
                zeros(bp, n_s5), zeros(bp, n_s5), lw, tm_p, min(seq_p, S5_TIME_BLOCK))
            shift_p.append(sh); wkv_p.append(wk); s5re_p.append(hr); s5im_p.append(hi)
            parts_s, sh, wk, hr, hi = _mixer_ab(
                ys, bs, seq_s, state_shift[j], state_wkv[j], state_s5_re[j], state_s5_im[j], lw, tm_s, 1)
            shift_s.append(sh); wkv_s.append(wk); s5re_s.append(hr); s5im_s.append(hi)
        else:
            w_c = w_in_c[j].astype(BF16)
            w_qkv, w_f = w_c[:, :3 * d_model], w_c[:, 3 * d_model:]
            b_f = row(b_forget[j])
            w_out = w_out_c[j].astype(BF16)
            q, k, v, kb, vb, lf, c = _fox_proj(yp, w_qkv, w_f, b_f, seq_p, tm_p)
            c = jnp.swapaxes(c.reshape(bp, seq_p, n_heads) * LOG2E, 1, 2).reshape(bp, n_heads, 1, seq_p)
            o = _fox_attn(q.reshape(bp, seq_p, d_model), kb.reshape(bp, seq_p, d_model),
                          vb.reshape(bp, seq_p, d_model), c, min(seq_p, ATTN_Q_TILE))
            parts_p = (o.reshape(mp, d_model),)
            if k.shape == (bp, d_model, seq_p):
                heads_last = lambda z: jnp.transpose(z.reshape(bp, n_heads, HEAD, seq_p), (0, 3, 1, 2))
            else:
                heads_last = lambda z: z.reshape(bp, seq_p, n_heads, HEAD)
            k_p.append(heads_last(k)); v_p.append(heads_last(v))
            lf_p.append(lf.reshape(bp, seq_p, n_heads))
            q, k, v, _, _, lf, _ = _fox_proj(ys, w_qkv, w_f, b_f, seq_s, tm_s)
            o = _fox_decode(q.astype(F32), k, v, lf, cache_k, cache_v, cache_logf, page_table, j, DECODE_PAGES)
            parts_s = (o,)
            k_s.append(k.reshape(bs, seq_s, n_heads, HEAD)); v_s.append(v.reshape(bs, seq_s, n_heads, HEAD))
            lf_s.append(lf.reshape(bs, seq_s, n_heads))
        g1, b1, g2, b2 = row(ln1_g[layer]), row(ln1_b[layer]), row(ln2_g[layer]), row(ln2_b[layer])
        wg, wu, wd = ffn_w_gate[layer].astype(BF16), ffn_w_up[layer].astype(BF16), ffn_w_down[layer].astype(BF16)
        yp = _outproj_ln(parts_p, w_out, yp, g1, b1, alpha, tm_p)
        ys = _outproj_ln(parts_s, w_out, ys, g1, b1, alpha, tm_s)
        yp = _ffn_ln(yp, wg, wu, wd, g2, b2, alpha, tm_p)
        ys = _ffn_ln(ys, wg, wu, wd, g2, b2, alpha, tm_s)
    return (yp.reshape(bp, seq_p, d_model), ys.reshape(bs, seq_s, d_model),
            jnp.stack(shift_p), jnp.stack(shift_s), jnp.stack(wkv_p), jnp.stack(wkv_s),
            jnp.stack(s5re_p), jnp.stack(s5re_s), jnp.stack(s5im_p), jnp.stack(s5im_s),
            jnp.stack(k_p), jnp.stack(k_s), jnp.stack(v_p), jnp.stack(v_s),
            jnp.stack(lf_p), jnp.stack(lf_s))
```

```python
import functools
import math

import jax
import jax.numpy as jnp
from jax import lax
from jax.experimental import pallas as pl
from jax.experimental.pallas import tpu as pltpu

F32, BF16 = jnp.float32, jnp.bfloat16
HI = lax.Precision.HIGHEST

LANES = 128
SUBLANES = 8
VMEM_LIMIT_BYTES = 56 * 1024 * 1024

HEAD = 64
S5_GROUP = 16
S5_STATE = 64
LORA_WA = 128
LORA_G = 128
GN_EPS = 64e-5
LN_EPS = 1e-5
RWKV_CHUNK = 64
LOG2E = math.log2(math.e)

ROW_TILE = 512
S5_TIME_BLOCK = 128
ATTN_Q_TILE = 1024
ATTN_ROW_BLOCK = 512
DECODE_PAGES = 8


def _dot(a, b, precision=None):
    return jnp.dot(a, b, preferred_element_type=F32, precision=precision)


def _dot_nt(a, b, precision=None):
    return lax.dot_general(a, b, (((1,), (1,)), ((), ())), preferred_element_type=F32, precision=precision)


def _dot_tn(a, b, precision=None):
    return lax.dot_general(a, b, (((0,), (0,)), ((), ())), preferred_element_type=F32, precision=precision)


def _split(a, terms=2):
    out = []
    for _ in range(terms - 1):
        piece = a.astype(BF16)
        out.append(piece)
        a = a - piece.astype(F32)
    out.append(a.astype(BF16))
    return tuple(out)


def _mm3(a2, b2, dot=_dot):
    (a_hi, a_lo), (b_hi, b_lo) = a2, b2
    return dot(a_hi, b_hi) + dot(a_hi, b_lo) + dot(a_lo, b_hi)


def _mm1(a2, b2, dot=_dot):
    return dot(a2[0], b2[0])


def _mm_exact_rhs(a_terms, b):
    acc = None
    for piece in a_terms:
        t = _dot(piece, b)
        acc = t if acc is None else acc + t
    return acc


def _params(*semantics):
    return pltpu.CompilerParams(dimension_semantics=semantics, vmem_limit_bytes=VMEM_LIMIT_BYTES)


def _row_tile(m, preferred):
    tm = min(m, preferred)
    assert m % tm == 0, (m, tm)
    return tm


def _layer_norm(z, g, b):
    mu = jnp.mean(z, axis=-1, keepdims=True)
    zc = z - mu
    var = jnp.mean(zc * zc, axis=-1, keepdims=True)
    return zc * lax.rsqrt(var + LN_EPS) * g + b


def _softplus(z):
    return jnp.maximum(z, 0.0) + jnp.log1p(jnp.exp(-jnp.abs(z)))


def _proj_kernel(x_ref, w_ref, *o_refs, widths):
    acc = _dot(x_ref[...].astype(BF16), w_ref[...])
    off = 0
    for o_ref, n in zip(o_refs, widths):
        o_ref[...] = acc[:, off:off + n].astype(o_ref.dtype)
        off += n


def _proj(x, w, widths, dtypes, tm):
    m, k = x.shape
    n = w.shape[1]
    assert sum(widths) == n
    return pl.pallas_call(
        functools.partial(_proj_kernel, widths=tuple(widths)),
        grid=(m // tm,),
        in_specs=[pl.BlockSpec((tm, k), lambda i: (i, 0)), pl.BlockSpec((k, n), lambda i: (0, 0))],
        out_specs=[pl.BlockSpec((tm, wd), lambda i: (i, 0)) for wd in widths],
        out_shape=[jax.ShapeDtypeStruct((m, wd), dt) for wd, dt in zip(widths, dtypes)],
        compiler_params=_params("parallel"),
        name="proj_split",
    )(x, w)


def _outproj_ln_kernel(*refs, n_parts, alpha):
    part_refs = refs[:n_parts]
    w_ref, x_ref, g_ref, b_ref, o_ref = refs[n_parts:]
    off = 0
    mix = None
    for p_ref in part_refs:
        kp = p_ref.shape[1]
        t = _dot(p_ref[...].astype(BF16), w_ref[off:off + kp, :])
        mix = t if mix is None else mix + t
        off += kp
    o_ref[...] = _layer_norm(alpha * x_ref[...] + mix, g_ref[...], b_ref[...])


def _outproj_ln(parts, w, x, g, b, alpha, tm):
    m, d = x.shape
    in_specs = [pl.BlockSpec((tm, p.shape[1]), lambda i: (i, 0)) for p in parts]
    in_specs += [pl.BlockSpec(w.shape, lambda i: (0, 0)), pl.BlockSpec((tm, d), lambda i: (i, 0)),
                 pl.BlockSpec((1, d), lambda i: (0, 0)), pl.BlockSpec((1, d), lambda i: (0, 0))]
    return pl.pallas_call(
        functools.partial(_outproj_ln_kernel, n_parts=len(parts), alpha=alpha),
        grid=(m // tm,),
        in_specs=in_specs,
        out_specs=pl.BlockSpec((tm, d), lambda i: (i, 0)),
        out_shape=jax.ShapeDtypeStruct((m, d), F32),
        compiler_params=_params("parallel"),
        name="outproj_ln",
    )(*parts, w, x, g, b)


def _ffn_chunk(dff):
    for c in (512, 256, 128):
        if dff % c == 0:
            return c
    return dff


def _ffn_ln_kernel(x_ref, wg_ref, wu_ref, wd_ref, g_ref, b_ref, o_ref, *, alpha, chunk):
    x = x_ref[...]
    xb = x.astype(BF16)
    dff = wg_ref.shape[1]
    out = None
    for c in range(dff // chunk):
        cs = slice(c * chunk, (c + 1) * chunk)
        gate = _dot(xb, wg_ref[:, cs])
        up = _dot(xb, wu_ref[:, cs])
        h = (gate * jax.nn.sigmoid(gate) * up).astype(BF16)
        t = _dot(h, wd_ref[cs, :])
        out = t if out is None else out + t
    o_ref[...] = _layer_norm(alpha * x + out, g_ref[...], b_ref[...])


def _ffn_ln(x, wg, wu, wd, g, b, alpha, tm):
    m, d = x.shape
    dff = wg.shape[1]
    const = lambda i: (0, 0)
    return pl.pallas_call(
        functools.partial(_ffn_ln_kernel, alpha=alpha, chunk=_ffn_chunk(dff)),
        grid=(m // tm,),
        in_specs=[pl.BlockSpec((tm, d), lambda i: (i, 0)), pl.BlockSpec((d, dff), const),
                  pl.BlockSpec((d, dff), const), pl.BlockSpec((dff, d), const),
                  pl.BlockSpec((1, d), const), pl.BlockSpec((1, d), const)],
        out_specs=pl.BlockSpec((tm, d), lambda i: (i, 0)),
        out_shape=jax.ShapeDtypeStruct((m, d), F32),
        compiler_params=_params("parallel"),
        name="ffn_ln",
    )(x, wg, wu, wd, g, b)


def _rwkv_kernel(p_ref, sprev_ref, s0_ref, mu_ref, w0_ref, wlw_ref, a0_ref, wla_ref, wlg_ref, kk_ref, ka_ref,
                 rk_ref, gng_ref, gnb_ref, ones_ref, y_ref, shift_ref, sout_ref, carry_sc, state_sc,
                 *, chunk, chunks_per_step, t_total, n_steps):
    c = pl.program_id(1)
    d = y_ref.shape[-1]
    n_pairs = d // LANES
    ell = chunk
    cps = chunks_per_step
    rows = cps * ell

    @pl.when(c == 0)
    def _():
        carry_sc[...] = sprev_ref[...]
        state_sc[...] = s0_ref[...]

    p = p_ref[...]
    row = lax.broadcasted_iota(jnp.int32, (rows, 1), 0)
    prev = jnp.where(row == 0, carry_sc[...], pltpu.roll(p, 1, 0))
    ps = p + (prev - p) * mu_ref[...]
    carry_sc[...] = p[rows - 1:rows, :]

    r = ps[:, 0:d]
    k = ps[:, d:2 * d]
    v = ps[:, 2 * d:3 * d]
    x_wa = ps[:, 3 * d:3 * d + LORA_WA]
    x_g = ps[:, 3 * d + LORA_WA:3 * d + LORA_WA + LORA_G]
    ones_blk = ones_ref[...]
    head_sum = lambda z: _mm_exact_rhs(_split(z), ones_blk)

    w_log = -_softplus(-(w0_ref[...] + _mm3(_split(jnp.tanh(x_wa)), (wlw_ref[0], wlw_ref[1])))) - 0.5
    logw = -jnp.exp(w_log)
    a = jax.nn.sigmoid(a0_ref[...] + _mm3(_split(x_wa), (wla_ref[0], wla_ref[1])))
    g = _mm3(_split(jax.nn.sigmoid(x_g)), (wlg_ref[0], wlg_ref[1]))
    kk = k * kk_ref[...]
    kk = kk / jnp.maximum(jnp.sqrt(head_sum(kk * kk)), 1e-12)
    k2 = k * (1.0 + (a - 1.0) * ka_ref[...])
    bv = kk * a
    if t_total % rows:
        valid = (c * rows + row) < t_total
        logw = jnp.where(valid, logw, 0.0)
        kk = jnp.where(valid, kk, 0.0)
        bv = jnp.where(valid, bv, 0.0)
        k2 = jnp.where(valid, k2, 0.0)

    ti = lax.broadcasted_iota(jnp.int32, (rows, rows), 0)
    tj = lax.broadcasted_iota(jnp.int32, (rows, rows), 1)
    tri = ((ti >= tj) & (ti // ell == tj // ell)).astype(BF16)
    cl = None
    for piece in _split(logw, 3):
        t = _dot(tri, piece)
        cl = t if cl is None else cl + t
    e_in = jnp.exp(cl)
    e_neg = jnp.exp(-cl)
    a_t = _split(-kk * jnp.exp(cl - logw))
    r_t = _split(r * e_in)
    b_t = _split(bv * e_neg)
    k_t = _split(k2 * e_neg)
    v_t = _split(v)

    lane = lax.broadcasted_iota(jnp.int32, (1, LANES), 1)
    first = lane < HEAD
    si = lax.broadcasted_iota(jnp.int32, (2 * ell, 2 * ell), 0) % ell
    sj = lax.broadcasted_iota(jnp.int32, (2 * ell, 2 * ell), 1) % ell
    strict = si > sj
    incl = si >= sj
    eye = (lax.broadcasted_iota(jnp.int32, (2 * ell, 2 * ell), 0)
           == lax.broadcasted_iota(jnp.int32, (2 * ell, 2 * ell), 1)).astype(F32)

    def stack(z2, ch, sl):
        zero = jnp.zeros((ell, LANES), BF16)
        tok = slice(ch * ell, (ch + 1) * ell)
        return tuple(jnp.concatenate([jnp.where(first, z[tok, sl], zero), jnp.where(first, zero, z[tok, sl])], axis=0)
                     for z in z2)

    cat = lambda x2, y2: tuple(jnp.concatenate([x, y], axis=0) for x, y in zip(x2, y2))

    pairs = range(n_pairs)
    sls = [slice(pr * LANES, (pr + 1) * LANES) for pr in pairs]
    units = [(ch, pr) for ch in range(cps) for pr in pairs]
    a_s = {u_: stack(a_t, u_[0], sls[u_[1]]) for u_ in units}
    r_s = {u_: stack(r_t, u_[0], sls[u_[1]]) for u_ in units}
    v_s = {u_: stack(v_t, u_[0], sls[u_[1]]) for u_ in units}
    bk_s = {u_: cat(stack(b_t, u_[0], sls[u_[1]]), stack(k_t, u_[0], sls[u_[1]])) for u_ in units}
    gram = {u_: _mm1(cat(a_s[u_], r_s[u_]), bk_s[u_], _dot_nt) for u_ in units}
    n_ab = {u_: jnp.where(strict, gram[u_][:2 * ell, :2 * ell], 0.0) for u_ in units}
    n_ak = {u_: jnp.where(strict, gram[u_][:2 * ell, 2 * ell:], 0.0) for u_ in units}
    n_rb = {u_: jnp.where(incl, gram[u_][2 * ell:, :2 * ell], 0.0) for u_ in units}
    n_rk = {u_: jnp.where(incl, gram[u_][2 * ell:, 2 * ell:], 0.0) for u_ in units}
    inv = {u_: eye + n_ab[u_] for u_ in units}
    pw = dict(n_ab)
    for _ in range(int(math.log2(ell)) - 1):
        pw2 = {u_: _split(pw[u_]) for u_ in units}
        pw = {u_: _mm3(pw2[u_], pw2[u_]) for u_ in units}
        inv = {u_: inv[u_] + _mm3(_split(pw[u_]), _split(inv[u_])) for u_ in units}
    state = [state_sc[pr] for pr in pairs]
    y_chunks = []
    for ch in range(cps):
        tok = slice(ch * ell, (ch + 1) * ell)
        w_end = e_in[(ch + 1) * ell - 1:(ch + 1) * ell, :]
        state2 = [_split(s) for s in state]
        x = [_mm1(a_s[ch, pr], state2[pr], _dot_nt) + _mm1(_split(n_ak[ch, pr]), v_s[ch, pr]) for pr in pairs]
        u = [_mm3(_split(inv[ch, pr]), _split(x[pr])) for pr in pairs]
        y2 = [_mm1(r_s[ch, pr], state2[pr], _dot_nt) + _mm1(_split(n_rb[ch, pr]), _split(u[pr]))
              + _mm1(_split(n_rk[ch, pr]), v_s[ch, pr]) for pr in pairs]
        new_state = []
        for pr in pairs:
            sl = sls[pr]
            v_f32 = jnp.concatenate([jnp.where(first, v[tok, sl], 0.0), jnp.where(first, 0.0, v[tok, sl])], axis=0)
            uv_t = jnp.concatenate([u[pr], v_f32], axis=0).T
            new_state.append((state[pr] + _mm1(_split(uv_t), bk_s[ch, pr])) * w_end[:, sl])
        state = new_state
        y_chunks.append(jnp.concatenate([m[:ell] + m[ell:] for m in y2], axis=1))
    for pr in pairs:
        state_sc[pr] = state[pr]

    y = jnp.concatenate(y_chunks, axis=0)
    inv_head = 1.0 / HEAD
    mean = head_sum(y) * inv_head
    yc = y - mean
    var = head_sum(yc * yc) * inv_head
    yn = yc * lax.rsqrt(var + GN_EPS) * gng_ref[...] + gnb_ref[...]
    bonus = head_sum(r * k2 * rk_ref[...]) * v
    y_ref[...] = ((yn + bonus) * g).astype(y_ref.dtype)

    @pl.when(c == n_steps - 1)
    def _():
        last = (t_total - 1) % rows
        shift_ref[...] = p[last:last + 1, :]
        sout_ref[...] = state_sc[...]


def _pad_rows(w, rows, offset):
    out = jnp.zeros((rows, w.shape[1]), F32)
    return out.at[offset:offset + w.shape[0]].set(w.astype(F32))


def _rwkv(p, shift_prev, wkv0, lw):
    bsz, t_total, width = p.shape
    d = width - LORA_WA - LORA_G
    d //= 3
    n_pairs = d // LANES
    ell = RWKV_CHUNK
    n_chunks = -(-t_total // ell)
    cps = next(n for n in (4, 2, 1) if n_chunks % n == 0)
    rows = cps * ell
    n_steps = n_chunks // cps
    t_pad = n_chunks * ell
    if t_pad != t_total:
        p = jnp.pad(p, ((0, 0), (0, t_pad - t_total), (0, 0)))
    s0 = wkv0.astype(F32).reshape(bsz, n_pairs, 2, HEAD, HEAD)
    zero = jnp.zeros_like(s0[:, :, 0])
    s0 = jnp.concatenate([jnp.concatenate([s0[:, :, 0], zero], axis=-1),
                          jnp.concatenate([zero, s0[:, :, 1]], axis=-1)], axis=-2)
    row = lambda z: z.astype(F32).reshape(1, -1)
    ones_blk = jnp.kron(jnp.eye(d // HEAD, dtype=F32), jnp.ones((HEAD, HEAD), F32)).astype(BF16)
    hi_lo = lambda w: jnp.stack(_split(w))
    consts = [row(lw["mu"]), row(lw["w0"]), hi_lo(_pad_rows(lw["w_lora_w"], LORA_WA, 0)), row(lw["a0"]),
              hi_lo(_pad_rows(lw["w_lora_a"], LORA_WA, LORA_WA // 2)), hi_lo(lw["w_lora_g"].astype(F32)),
              row(lw["k_k"]), row(lw["k_a"]), row(lw["r_k"]), row(lw["lnx_g"]), row(lw["lnx_b"]), ones_blk]
    const_specs = [pl.BlockSpec(cst.shape, lambda b, c, nd=cst.ndim: (0,) * nd) for cst in consts]
    y, shift, sout = pl.pallas_call(
        functools.partial(_rwkv_kernel, chunk=ell, chunks_per_step=cps, t_total=t_total, n_steps=n_steps),
        grid=(bsz, n_steps),
        in_specs=[pl.BlockSpec((None, rows, width), lambda b, c: (b, c, 0)),
                  pl.BlockSpec((None, 1, width), lambda b, c: (b, 0, 0)),
                  pl.BlockSpec((None, n_pairs, LANES, LANES), lambda b, c: (b, 0, 0, 0))] + const_specs,
        out_specs=[pl.BlockSpec((None, rows, d), lambda b, c: (b, c, 0)),
                   pl.BlockSpec((None, 1, width), lambda b, c: (b, 0, 0)),
                   pl.BlockSpec((None, n_pairs, LANES, LANES), lambda b, c: (b, 0, 0, 0))],
        out_shape=[jax.ShapeDtypeStruct((bsz, t_pad, d), BF16),
                   jax.ShapeDtypeStruct((bsz, 1, width), F32),
                   jax.ShapeDtypeStruct((bsz, n_pairs, LANES, LANES), F32)],
        scratch_shapes=[pltpu.VMEM((1, width), F32), pltpu.VMEM((n_pairs, LANES, LANES), F32)],
        compiler_params=_params("parallel", "arbitrary"),
        name="rwkv7_chunked",
    )(p, shift_prev.astype(F32).reshape(bsz, 1, width), s0, *consts)
    wkv = jnp.stack([sout[:, :, :HEAD, :HEAD], sout[:, :, HEAD:, HEAD:]], axis=2)
    return y[:, :t_total], shift[:, 0], wkv.reshape(bsz, 2 * n_pairs, HEAD, HEAD)


def _gelu_tanh(x):
    return 0.5 * x * (1.0 + jnp.tanh(math.sqrt(2.0 / math.pi) * (x + 0.044715 * (x * x * x))))


def _s5_kernel(u_ref, h0re_ref, h0im_ref, abre_ref, abim_ref, wbre_ref, wbim_ref, wcre_ref, wcim_ref, dskip_ref,
               wglu_ref, bglu_ref, y_ref, hre_ref, him_ref, ere_sc, eim_sc, hre_sc, him_sc,
               *, tb, n_blocks, blocks_per_group):
    tblk = pl.program_id(1)
    nb = SUBLANES
    n_state = hre_sc.shape[1]
    n_in = u_ref.shape[-1]
    n_panels = wbre_ref.shape[0]
    p_in, p_state = n_in // n_panels, n_state // n_panels

    @pl.when(tblk == 0)
    def _():
        hre_sc[...] = h0re_ref[...]
        him_sc[...] = h0im_ref[...]

    u = u_ref[...].reshape(tb * nb, n_in)
    ub = u.astype(BF16)
    for pn in range(n_panels):
        up = ub[:, pn * p_in:(pn + 1) * p_in]
        ere_sc[:, pn * p_state:(pn + 1) * p_state] = _dot(up, wbre_ref[pn])
        eim_sc[:, pn * p_state:(pn + 1) * p_state] = _dot(up, wbim_ref[pn])

    n_blk = n_state // LANES
    for cg in range(n_blk // blocks_per_group):
        blks = range(cg * blocks_per_group, (cg + 1) * blocks_per_group)
        a_re = [jnp.broadcast_to(abre_ref[:, jb * LANES:(jb + 1) * LANES], (nb, LANES)) for jb in blks]
        a_im = [jnp.broadcast_to(abim_ref[:, jb * LANES:(jb + 1) * LANES], (nb, LANES)) for jb in blks]

        def step(t, carry, blks=blks, a_re=a_re, a_im=a_im):
            rows = pl.ds(pl.multiple_of(t * nb, nb), nb)
            new_re, new_im = [], []
            for n, jb in enumerate(blks):
                cs = slice(jb * LANES, (jb + 1) * LANES)
                h_re, h_im = carry[0][n], carry[1][n]
                n_re = a_re[n] * h_re - a_im[n] * h_im + ere_sc[rows, cs]
                n_im = a_re[n] * h_im + a_im[n] * h_re + eim_sc[rows, cs]
                ere_sc[rows, cs] = n_re
                eim_sc[rows, cs] = n_im
                new_re.append(n_re)
                new_im.append(n_im)
            return tuple(new_re), tuple(new_im)

        init = (tuple(hre_sc[:, jb * LANES:(jb + 1) * LANES] for jb in blks),
                tuple(him_sc[:, jb * LANES:(jb + 1) * LANES] for jb in blks))
        fin_re, fin_im = lax.fori_loop(0, tb, step, init, unroll=2 if tb % 2 == 0 else 1)
        for n, jb in enumerate(blks):
            hre_sc[:, jb * LANES:(jb + 1) * LANES] = fin_re[n]
            him_sc[:, jb * LANES:(jb + 1) * LANES] = fin_im[n]

    y_parts = []
    for pn in range(n_panels):
        ss = slice(pn * p_state, (pn + 1) * p_state)
        y_parts.append(_dot(ere_sc[:, ss].astype(BF16), wcre_ref[pn]) - _dot(eim_sc[:, ss].astype(BF16), wcim_ref[pn]))
    y = jnp.concatenate(y_parts, axis=1) + dskip_ref[...] * u
    y = _gelu_tanh(y)
    y = y * jax.nn.sigmoid(_dot(y.astype(BF16), wglu_ref[...]) + bglu_ref[...])
    y_ref[...] = y.reshape(tb, nb, n_in).astype(y_ref.dtype)

    @pl.when(tblk == n_blocks - 1)
    def _():
        hre_ref[...] = hre_sc[...]
        him_ref[...] = him_sc[...]


def _s5_weights(lw):
    lr, li = lw["s5_lambda_re"].astype(F32), lw["s5_lambda_im"].astype(F32)
    n_groups = lr.shape[0]
    dt = jnp.exp(lw["s5_log_dt"].astype(F32))[:, None]
    mag = jnp.exp(lr * dt)
    ab_re, ab_im = mag * jnp.cos(li * dt), mag * jnp.sin(li * dt)
    den = lr * lr + li * li
    n_re = ab_re - 1.0
    f_re = (n_re * lr + ab_im * li) / den
    f_im = (ab_im * lr - n_re * li) / den
    br, bi = lw["s5_b_re"].astype(F32), lw["s5_b_im"].astype(F32)
    bb_re = f_re[..., None] * br - f_im[..., None] * bi
    bb_im = f_re[..., None] * bi + f_im[..., None] * br
    eye = jnp.eye(n_groups, dtype=F32)
    n_in, n_state = n_groups * S5_GROUP, n_groups * S5_STATE
    n_panels = max(1, n_in // (2 * LANES))
    p_in, p_state = n_in // n_panels, n_state // n_panels

    def panels(w, rows, cols):
        return jnp.stack([w[pn * rows:(pn + 1) * rows, pn * cols:(pn + 1) * cols] for pn in range(n_panels)])

    expand_b = lambda z: panels(jnp.einsum("gpc,gh->gchp", z, eye).reshape(n_in, n_state), p_in, p_state).astype(BF16)
    expand_c = lambda z: panels(jnp.einsum("gcp,gh->gphc", z.astype(F32), eye).reshape(n_state, n_in),
                                p_state, p_in).astype(BF16)
    return dict(ab_re=ab_re.reshape(1, n_state), ab_im=ab_im.reshape(1, n_state),
                wb_re=expand_b(bb_re), wb_im=expand_b(bb_im),
                wc_re=expand_c(lw["s5_c_re"]), wc_im=expand_c(lw["s5_c_im"]),
                d_skip=lw["s5_d"].astype(F32).reshape(1, n_in),
                w_glu=lw["s5_w_glu"].astype(BF16), b_glu=lw["s5_b_glu"].astype(F32).reshape(1, n_in))


def _s5(u, h_re0, h_im0, sw, tb):
    bsz, t_total, n_in = u.shape
    n_state = sw["ab_re"].shape[1]
    nb = SUBLANES
    assert bsz % nb == 0 and t_total % tb == 0
    n_blocks = t_total // tb
    blocks_per_group = math.gcd(n_state // LANES, 8)
    consts = [sw["ab_re"], sw["ab_im"], sw["wb_re"], sw["wb_im"], sw["wc_re"], sw["wc_im"], sw["d_skip"],
              sw["w_glu"], sw["b_glu"]]
    seq_block = pl.BlockSpec((tb, nb, n_in), lambda g, t: (t, g, 0))
    state_block = pl.BlockSpec((nb, n_state), lambda g, t: (g, 0))
    y, h_re, h_im = pl.pallas_call(
        functools.partial(_s5_kernel, tb=tb, n_blocks=n_blocks, blocks_per_group=blocks_per_group),
        grid=(bsz // nb, n_blocks),
        in_specs=[seq_block, state_block, state_block]
                 + [pl.BlockSpec(cst.shape, lambda g, t, nd=cst.ndim: (0,) * nd) for cst in consts],
        out_specs=[seq_block, state_block, state_block],
        out_shape=[jax.ShapeDtypeStruct((t_total, bsz, n_in), BF16),
                   jax.ShapeDtypeStruct((bsz, n_state), F32),
                   jax.ShapeDtypeStruct((bsz, n_state), F32)],
        scratch_shapes=[pltpu.VMEM((nb * tb, n_state), F32), pltpu.VMEM((nb * tb, n_state), F32),
                        pltpu.VMEM((nb, n_state), F32), pltpu.VMEM((nb, n_state), F32)],
        compiler_params=_params("parallel", "arbitrary"),
        name="s5_scan",
    )(jnp.swapaxes(u, 0, 1), h_re0.astype(F32).reshape(bsz, n_state), h_im0.astype(F32).reshape(bsz, n_state),
      *consts)
    shape = (bsz, n_state // S5_STATE, S5_STATE)
    return jnp.swapaxes(y, 0, 1), h_re.reshape(shape), h_im.reshape(shape)


def _log_sigmoid(z):
    return jnp.minimum(z, 0.0) - jnp.log1p(jnp.exp(-jnp.abs(z)))


def _fox_proj_kernel(x_ref, w_ref, wf_ref, bf_ref, q_ref, k_ref, v_ref, kb_ref, vb_ref, lf_ref, c_ref, carry_sc,
                     *, d, scale, tiles_per_seq, channel_major):
    i = pl.program_id(0)
    tm = x_ref.shape[0]
    xb = x_ref[...].astype(BF16)
    acc = _dot(xb, w_ref[...])
    q_ref[...] = (acc[:, :d] * scale).astype(BF16)
    k = acc[:, d:2 * d]
    v = acc[:, 2 * d:3 * d]
    if channel_major:
        k_ref[...] = k.T
        v_ref[...] = v.T
    else:
        k_ref[...] = k
        v_ref[...] = v
    kb_ref[...] = k.astype(BF16)
    vb_ref[...] = v.astype(BF16)
    lf = _log_sigmoid(_dot(xb, wf_ref[...]) + bf_ref[...])
    lf_ref[...] = lf

    @pl.when(i % tiles_per_seq == 0)
    def _():
        carry_sc[...] = jnp.zeros_like(carry_sc)

    ti = lax.broadcasted_iota(jnp.int32, (tm, tm), 0)
    tj = lax.broadcasted_iota(jnp.int32, (tm, tm), 1)
    tri = (ti >= tj).astype(BF16)
    c = carry_sc[...]
    for piece in _split(lf, 3):
        c = c + _dot(tri, piece)
    c_ref[...] = c
    carry_sc[...] = c[tm - 1:tm, :]


def _fox_proj(x, w_qkv, w_f, b_f, seq_len, tm):
    m, kdim = x.shape
    d = w_qkv.shape[1] // 3
    nh = w_f.shape[1]
    assert seq_len % tm == 0 or tm % seq_len == 0
    tiles_per_seq = max(seq_len // tm, 1)
    channel_major = seq_len % tm == 0 and tm % LANES == 0
    row = lambda n: pl.BlockSpec((tm, n), lambda i: (i, 0))
    if channel_major:
        kv_spec = pl.BlockSpec((None, d, tm), lambda i: (i // tiles_per_seq, 0, i % tiles_per_seq))
        kv_shape = jax.ShapeDtypeStruct((m // seq_len, d, seq_len), F32)
    else:
        kv_spec, kv_shape = row(d), jax.ShapeDtypeStruct((m, d), F32)
    return pl.pallas_call(
        functools.partial(_fox_proj_kernel, d=d, scale=HEAD ** -0.5 * LOG2E, tiles_per_seq=tiles_per_seq,
                          channel_major=channel_major),
        grid=(m // tm,),
        in_specs=[row(kdim), pl.BlockSpec(w_qkv.shape, lambda i: (0, 0)), pl.BlockSpec(w_f.shape, lambda i: (0, 0)),
                  pl.BlockSpec((1, nh), lambda i: (0, 0))],
        out_specs=[row(d), kv_spec, kv_spec, row(d), row(d), row(nh), row(nh)],
        out_shape=[jax.ShapeDtypeStruct((m, d), BF16), kv_shape, kv_shape, jax.ShapeDtypeStruct((m, d), BF16),
                   jax.ShapeDtypeStruct((m, d), BF16), jax.ShapeDtypeStruct((m, nh), F32),
                   jax.ShapeDtypeStruct((m, nh), F32)],
        scratch_shapes=[pltpu.VMEM((1, nh), F32)],
        compiler_params=_params("arbitrary"),
        name="fox_proj",
    )(x, w_qkv, w_f, b_f)


def _fox_attn_kernel(q_ref, k_ref, v_ref, c_ref, o_ref, m_sc, l_sc, acc_sc, *, tq, rq):
    qi = pl.program_id(2)
    n_rb = tq // rq
    heads = LANES // HEAD
    lane = lax.broadcasted_iota(jnp.int32, (1, LANES), 1)
    q = q_ref[...]
    qm = [jnp.where((lane // HEAD) == h, q, jnp.zeros_like(q)) for h in range(heads)]
    m_sc[...] = jnp.full_like(m_sc, -jnp.inf)
    l_sc[...] = jnp.zeros_like(l_sc)
    acc_sc[...] = jnp.zeros_like(acc_sc)

    def chain(h, rb, start, width, masked):
        rows = slice(rb * rq, (rb + 1) * rq)
        kt = k_ref[pl.ds(start, width), :]
        vt = v_ref[pl.ds(start, width), :]
        s = _dot_nt(qm[h][rows], kt) - c_ref[h, :, pl.ds(start, width)]
        if masked:
            ri = lax.broadcasted_iota(jnp.int32, (rq, width), 0) + rb * rq
            ci = lax.broadcasted_iota(jnp.int32, (rq, width), 1)
            s = jnp.where(ci <= ri, s, -jnp.inf)
        m_prev = m_sc[h, rows, :]
        m_new = jnp.maximum(m_prev, jnp.max(s, axis=1, keepdims=True))
        alpha = jnp.exp2(m_prev - m_new)
        p = jnp.exp2(s - jnp.concatenate([m_new] * (width // LANES), axis=1))
        l_sc[h, rows, :] = alpha * l_sc[h, rows, :] + jnp.sum(p, axis=1, keepdims=True)
        acc_sc[h, rows, :] = alpha * acc_sc[h, rows, :] + _dot(p.astype(BF16), vt)
        m_sc[h, rows, :] = m_new

    def body(ki, carry):
        start = pl.multiple_of(ki * tq, tq)
        for h in range(heads):
            for rb in range(n_rb):
                chain(h, rb, start, tq, False)
        return carry

    lax.fori_loop(0, qi, body, 0)
    start = pl.multiple_of(qi * tq, tq)
    for h in range(heads):
        for rb in range(n_rb):
            chain(h, rb, start, (rb + 1) * rq, True)
    out = [acc_sc[h] / l_sc[h] for h in range(heads)]
    o_ref[...] = jnp.where(lane < HEAD, out[0], out[1]).astype(o_ref.dtype)


def _fox_attn(q, kb, vb, c, tq):
    bsz, t_total, d = q.shape
    n_pairs = d // LANES
    hp = LANES // HEAD
    return pl.pallas_call(
        functools.partial(_fox_attn_kernel, tq=tq, rq=min(tq, ATTN_ROW_BLOCK)),
        grid=(bsz, n_pairs, t_total // tq),
        in_specs=[pl.BlockSpec((None, tq, LANES), lambda b, h, i: (b, i, h)),
                  pl.BlockSpec((None, t_total, LANES), lambda b, h, i: (b, 0, h)),
                  pl.BlockSpec((None, t_total, LANES), lambda b, h, i: (b, 0, h)),
                  pl.BlockSpec((None, hp, 1, t_total), lambda b, h, i: (b, h, 0, 0))],
        out_specs=pl.BlockSpec((None, tq, LANES), lambda b, h, i: (b, i, h)),
        out_shape=jax.ShapeDtypeStruct((bsz, t_total, d), BF16),
        scratch_shapes=[pltpu.VMEM((hp, tq, LANES), F32)] * 3,
        compiler_params=_params("parallel", "parallel", "arbitrary"),
        name="fox_attn",
    )(q, kb, vb, c)


def _fox_decode_kernel(pt_ref, q_ref, knew_ref, vnew_ref, cnew_ref, *refs, pages_per_step, n_steps):
    del pt_ref
    npp = pages_per_step
    k_refs, v_refs, lf_refs = refs[:npp], refs[npp:2 * npp], refs[2 * npp:3 * npp]
    o_ref, m_sc, l_sc, acc_sc, carry_sc = refs[3 * npp:]
    j = pl.program_id(1)
    nh, _, page = k_refs[0].shape

    @pl.when(j == 0)
    def _():
        m_sc[...] = jnp.full_like(m_sc, -jnp.inf)
        l_sc[...] = jnp.zeros_like(l_sc)
        acc_sc[...] = jnp.zeros_like(acc_sc)
        carry_sc[...] = cnew_ref[...]

    def scores(k_of_head):
        return jnp.concatenate([jnp.sum(q_ref[h] * k_of_head(h), axis=0, keepdims=True) for h in range(nh)], axis=0)

    later = (lax.broadcasted_iota(jnp.int32, (page, page), 0)
             > lax.broadcasted_iota(jnp.int32, (page, page), 1)).astype(F32)
    for i in range(npp):
        lft = lf_refs[i][...] * LOG2E
        bias = _dot(lft, later, HI) + carry_sc[...]
        carry_sc[...] = carry_sc[...] + jnp.sum(lft, axis=1, keepdims=True)
        s = scores(lambda h, i=i: k_refs[i][h]) + bias
        m_prev = m_sc[...]
        m_new = jnp.maximum(m_prev, jnp.max(s, axis=1, keepdims=True))
        alpha = jnp.exp2(m_prev - m_new)
        p = jnp.exp2(s - m_new)
        l_sc[...] = alpha * l_sc[...] + p
        for h in range(nh):
            acc_sc[h] = alpha[h:h + 1, :] * acc_sc[h] + p[h:h + 1, :] * v_refs[i][h]
        m_sc[...] = m_new

    @pl.when(j == n_steps - 1)
    def _():
        s_new = scores(lambda h: knew_ref[h])
        m_prev = m_sc[...]
        m_new = jnp.maximum(m_prev, s_new)
        alpha = jnp.exp2(m_prev - m_new)
        p_new = jnp.exp2(s_new - m_new)
        l_fin = alpha * jnp.sum(l_sc[...], axis=1, keepdims=True) + p_new
        for h in range(nh):
            num = (alpha[h:h + 1, :1] * jnp.sum(acc_sc[h], axis=1, keepdims=True)
                   + p_new[h:h + 1, :1] * vnew_ref[h])
            o_ref[h] = num / l_fin[h:h + 1, :1]


def _fox_decode(q, k_new, v_new, lf_new, cache_k, cache_v, cache_lf, page_table, layer, pages_per_step):
    bsz, d = q.shape
    nh = d // HEAD
    n_pages = page_table.shape[1]
    page = cache_k.shape[2]
    assert page == LANES
    npp = pages_per_step
    assert n_pages % npp == 0
    n_steps = n_pages // npp
    pt = page_table.reshape(-1).astype(jnp.int32)
    pool_k = jnp.transpose(cache_k, (0, 1, 3, 4, 2))
    pool_v = jnp.transpose(cache_v, (0, 1, 3, 4, 2))
    pool_lf = jnp.transpose(cache_lf, (0, 1, 3, 2))
    column = lambda z: z.astype(F32).reshape(bsz, nh, HEAD, 1)
    q_rep = jnp.broadcast_to(column(q), (bsz, nh, HEAD, LANES))
    c_new = jnp.broadcast_to((lf_new.astype(F32) * LOG2E)[:, :, None], (bsz, nh, LANES))

    def page_map(i, ndim):
        return lambda b, j, pt_ref: (layer, pt_ref[b * n_pages + (n_pages - 1 - (j * npp + i))]) + (0,) * ndim

    per_seq = lambda *blk: pl.BlockSpec((None,) + blk, lambda b, j, pt_ref: (b,) + (0,) * len(blk))
    in_specs = [per_seq(nh, HEAD, LANES), per_seq(nh, HEAD, 1), per_seq(nh, HEAD, 1), per_seq(nh, LANES)]
    in_specs += [pl.BlockSpec((None, None, nh, HEAD, page), page_map(i, 3)) for i in range(npp)]
    in_specs += [pl.BlockSpec((None, None, nh, HEAD, page), page_map(i, 3)) for i in range(npp)]
    in_specs += [pl.BlockSpec((None, None, nh, page), page_map(i, 2)) for i in range(npp)]
    out = pl.pallas_call(
        functools.partial(_fox_decode_kernel, pages_per_step=npp, n_steps=n_steps),
        grid_spec=pltpu.PrefetchScalarGridSpec(
            num_scalar_prefetch=1,
            grid=(bsz, n_steps),
            in_specs=in_specs,
            out_specs=per_seq(nh, HEAD, 1),
            scratch_shapes=[pltpu.VMEM((nh, LANES), F32), pltpu.VMEM((nh, LANES), F32),
                            pltpu.VMEM((nh, HEAD, LANES), F32), pltpu.VMEM((nh, LANES), F32)]),
        out_shape=jax.ShapeDtypeStruct((bsz, nh, HEAD, 1), F32),
        compiler_params=_params("parallel", "arbitrary"),
        name="fox_decode",
    )(pt, q_rep, column(k_new), column(v_new), c_new,
      *([pool_k] * npp), *([pool_v] * npp), *([pool_lf] * npp))
    return out.reshape(bsz, d)


def _mixer_ab(x2d, bsz, seq, shift_prev, wkv0, h_re0, h_im0, lw, tm, s5_tb):
    d_rwkv = lw["mu"].shape[0]
    d_s5 = lw["w_in"].shape[1] - d_rwkv
    p, u = _proj(x2d, lw["w_in"], (d_rwkv, d_s5), (F32, F32), tm)
    y_a, shift, wkv = _rwkv(p.reshape(bsz, seq, d_rwkv), shift_prev, wkv0, lw)
    y_b, h_re, h_im = _s5(u.reshape(bsz, seq, d_s5), h_re0, h_im0, lw["s5"], s5_tb)
    return (y_a.reshape(bsz * seq, -1), y_b.reshape(bsz * seq, -1)), shift, wkv, h_re, h_im


def kernel(x_prompt, x_sample, state_shift, state_wkv, state_s5_re, state_s5_im, cache_k, cache_v, cache_logf, page_table, w_in_ab, mu_shift, w0, w_lora_w, a0, w_lora_a, w_lora_g, k_k, k_a, r_k, lnx_g, lnx_b, s5_lambda_re, s5_lambda_im, s5_log_dt, s5_b_re, s5_b_im, s5_c_re, s5_c_im, s5_d, s5_w_glu, s5_b_glu, w_out_ab, w_in_c, b_forget, w_out_c, ln1_g, ln1_b, ln2_g, ln2_b, ffn_w_gate, ffn_w_up, ffn_w_down):
    bp, seq_p, d_model = x_prompt.shape
    bs, seq_s, _ = x_sample.shape
    assert seq_s == 1
    depth = ln1_g.shape[0]
    alpha = (2.0 * depth) ** 0.25
    n_heads = d_model // HEAD
    mp, ms = bp * seq_p, bs * seq_s
    tm_p, tm_s = _row_tile(mp, ROW_TILE), _row_tile(ms, ROW_TILE)
    row = lambda z: z.astype(F32).reshape(1, -1)

    yp = x_prompt.reshape(mp, d_model)
    ys = x_sample.reshape(ms, d_model)

    shift_p, shift_s, wkv_p, wkv_s = [], [], [], []
    s5re_p, s5re_s, s5im_p, s5im_s = [], [], [], []
    k_p, k_s, v_p, v_s, lf_p, lf_s = [], [], [], [], [], []
    for layer in range(depth):
        j = layer // 2
        if layer % 2 == 0:
            lw = dict(w_in=w_in_ab[j].astype(BF16), mu=mu_shift[j], w0=w0[j], w_lora_w=w_lora_w[j], a0=a0[j],
                      w_lora_a=w_lora_a[j], w_lora_g=w_lora_g[j], k_k=k_k[j], k_a=k_a[j], r_k=r_k[j],
                      lnx_g=lnx_g[j], lnx_b=lnx_b[j], s5_lambda_re=s5_lambda_re[j], s5_lambda_im=s5_lambda_im[j],
                      s5_log_dt=s5_log_dt[j], s5_b_re=s5_b_re[j], s5_b_im=s5_b_im[j], s5_c_re=s5_c_re[j],
                      s5_c_im=s5_c_im[j], s5_d=s5_d[j], s5_w_glu=s5_w_glu[j], s5_b_glu=s5_b_glu[j])
            lw["s5"] = _s5_weights(lw)
            w_out = w_out_ab[j].astype(BF16)
            n_s5 = lw["s5"]["ab_re"].shape[1]
            zeros = lambda *shape: jnp.zeros(shape, F32)
            parts_p, sh, wk, hr, hi = _mixer_ab(
                yp, bp, seq_p, zeros(bp, mu_shift.shape[1]), zeros(bp, (w0.shape[1]) // HEAD, HEAD, HEAD),
                zeros(bp, n_s5), zeros(bp, n_s5), lw, tm_p, min(seq_p, S5_TIME_BLOCK))
            shift_p.append(sh); wkv_p.append(wk); s5re_p.append(hr); s5im_p.append(hi)
            parts_s, sh, wk, hr, hi = _mixer_ab(
                ys, bs, seq_s, state_shift[j], state_wkv[j], state_s5_re[j], state_s5_im[j], lw, tm_s, 1)
            shift_s.append(sh); wkv_s.append(wk); s5re_s.append(hr); s5im_s.append(hi)
        else:
            w_c = w_in_c[j].astype(BF16)
            w_qkv, w_f = w_c[:, :3 * d_model], w_c[:, 3 * d_model:]
            b_f = row(b_forget[j])
            w_out = w_out_c[j].astype(BF16)
            q, k, v, kb, vb, lf, c = _fox_proj(yp, w_qkv, w_f, b_f, seq_p, tm_p)
            c = jnp.swapaxes(c.reshape(bp, seq_p, n_heads) * LOG2E, 1, 2).reshape(bp, n_heads, 1, seq_p)
            o = _fox_attn(q.reshape(bp, seq_p, d_model), kb.reshape(bp, seq_p, d_model),
                          vb.reshape(bp, seq_p, d_model), c, min(seq_p, ATTN_Q_TILE))
            parts_p = (o.reshape(mp, d_model),)
            if k.shape == (bp, d_model, seq_p):
                heads_last = lambda z: jnp.transpose(z.reshape(bp, n_heads, HEAD, seq_p), (0, 3, 1, 2))
            else:
                heads_last = lambda z: z.reshape(bp, seq_p, n_heads, HEAD)
            k_p.append(heads_last(k)); v_p.append(heads_last(v))
            lf_p.append(lf.reshape(bp, seq_p, n_heads))
            q, k, v, _, _, lf, _ = _fox_proj(ys, w_qkv, w_f, b_f, seq_s, tm_s)
            o = _fox_decode(q.astype(F32), k, v, lf, cache_k, cache_v, cache_logf, page_table, j, DECODE_PAGES)
            parts_s = (o,)
            k_s.append(k.reshape(bs, seq_s, n_heads, HEAD)); v_s.append(v.reshape(bs, seq_s, n_heads, HEAD))
            lf_s.append(lf.reshape(bs, seq_s, n_heads))
        g1, b1, g2, b2 = row(ln1_g[layer]), row(ln1_b[layer]), row(ln2_g[layer]), row(ln2_b[layer])
        wg, wu, wd = ffn_w_gate[layer].astype(BF16), ffn_w_up[layer].astype(BF16), ffn_w_down[layer].astype(BF16)
        yp = _outproj_ln(parts_p, w_out, yp, g1, b1, alpha, tm_p)
        ys = _outproj_ln(parts_s, w_out, ys, g1, b1, alpha, tm_s)
        yp = _ffn_ln(yp, wg, wu, wd, g2, b2, alpha, tm_p)
        ys = _ffn_ln(ys, wg, wu, wd, g2, b2, alpha, tm_s)
    return (yp.reshape(bp, seq_p, d_model), ys.reshape(bs, seq_s, d_model),
            jnp.stack(shift_p), jnp.stack(shift_s), jnp.stack(wkv_p), jnp.stack(wkv_s),
            jnp.stack(s5re_p), jnp.stack(s5re_s), jnp.stack(s5im_p), jnp.stack(s5im_s),
            jnp.stack(k_p), jnp.stack(k_s), jnp.stack(v_p), jnp.stack(v_s),
            jnp.stack(lf_p), jnp.stack(lf_s))
```
